```python
import math
import jax
import jax.numpy as jnp
from jax import lax
import numpy as np


D_MODEL = 4096
BATCH = 4
SEQ = 4096
DEPTH = 1

PLE_DIM = 256
D_FF = 11008
RMS_EPS = 1e-6

SSD_HEAD_DIM = 64
SSD_WIDTH = 5 * D_MODEL // 8
SSD_HEADS = SSD_WIDTH // SSD_HEAD_DIM
SSD_GROUPS = 8
SSD_STATE = 128
SSD_CONV = 4
SSD_CHUNK = 128
SSD_CONV_DIM = SSD_WIDTH + 2 * SSD_GROUPS * SSD_STATE

ATT_HEAD_DIM = 128
ATT_WIDTH = 3 * D_MODEL // 8
ATT_HEADS = ATT_WIDTH // ATT_HEAD_DIM
ATT_BRANCHES = ((128, 1), (512, 4), (2048, 16))
ATT_BLOCK = 128

MIX_WIDTH = SSD_WIDTH + ATT_WIDTH
IN_PROJ_WIDTH = SSD_WIDTH + SSD_CONV_DIM + SSD_HEADS + 3 * ATT_WIDTH

kernel_name = "hybrid_ssd_dilated_attn_macaron_ple"


def rms_norm(x, w):
    xf = x.astype(jnp.float32)
    y = xf * lax.rsqrt(jnp.mean(xf * xf, axis=-1, keepdims=True) + RMS_EPS)
    return (y * w.astype(jnp.float32)).astype(x.dtype)


def swiglu(x, w_gate, w_up, w_down):
    return (jax.nn.silu(x @ w_gate) * (x @ w_up)) @ w_down


def alibi_slopes(n):
    def pow2(m):
        start = 2.0 ** (-8.0 / m)
        return [start ** (i + 1) for i in range(m)]
    if math.log2(n).is_integer():
        s = pow2(n)
    else:
        c = 2 ** int(math.floor(math.log2(n)))
        s = pow2(c) + pow2(2 * c)[0::2][: n - c]
    return np.asarray(s, dtype=np.float32)


def causal_depthwise_conv(x, w, b):
    k_w, ch = w.shape
    y = lax.conv_general_dilated(
        x, w[:, None, :].astype(x.dtype), window_strides=(1,), padding=((k_w - 1, 0),),
        dimension_numbers=('NWC', 'WIO', 'NWC'), feature_group_count=ch)
    return y + b.astype(x.dtype)


def ssd_mixer(z, xbc, dt_raw, conv_w, conv_b, dt_bias, a_log, d_skip, norm_w):
    bsz, seq, _ = xbc.shape
    G, HG, P, N, Q = SSD_GROUPS, SSD_HEADS // SSD_GROUPS, SSD_HEAD_DIM, SSD_STATE, SSD_CHUNK
    nc = seq // Q
    xbc = jax.nn.silu(causal_depthwise_conv(xbc, conv_w, conv_b)).astype(jnp.float32)
    xs, bm, cm = jnp.split(xbc, [SSD_WIDTH, SSD_WIDTH + G * N], axis=-1)
    xs = xs.reshape(bsz, nc, Q, G, HG, P)
    bm = bm.reshape(bsz, nc, Q, G, N)
    cm = cm.reshape(bsz, nc, Q, G, N)
    dt = jax.nn.softplus(dt_raw.astype(jnp.float32) + dt_bias.astype(jnp.float32)).reshape(bsz, nc, Q, G, HG)
    a = -jnp.exp(a_log.astype(jnp.float32)).reshape(G, HG)
    a_cs = jnp.cumsum(jnp.moveaxis(dt * a, 2, -1), axis=-1)
    tril = jnp.tril(jnp.ones((Q, Q), dtype=bool))
    seg = a_cs[..., :, None] - a_cs[..., None, :]
    l_mat = jnp.exp(jnp.where(tril, seg, -jnp.inf))
    xdt = xs * dt[..., None]
    cb = jnp.einsum('bclgn,bcsgn->bcgls', cm, bm)
    y_diag = jnp.einsum('bcgls,bcgjls,bcsgjp->bclgjp', cb, l_mat, xdt)
    decay_states = jnp.exp(a_cs[..., -1:] - a_cs)
    states = jnp.einsum('bcsgn,bcgjs,bcsgjp->bcgjpn', bm, decay_states, xdt)
    chunk_decay = jnp.exp(a_cs[..., -1])

    def step(h, inp):
        st, dec = inp
        return h * dec[..., None, None] + st, h

    h0 = jnp.zeros((bsz, G, HG, P, N), jnp.float32)
    _, prev = lax.scan(step, h0, (jnp.moveaxis(states, 1, 0), jnp.moveaxis(chunk_decay, 1, 0)))
    prev = jnp.moveaxis(prev, 0, 1)
    y_off = jnp.einsum('bclgn,bcgjpn,bcgjl->bclgjp', cm, prev, jnp.exp(a_cs))
    y = y_diag + y_off + xs * d_skip.astype(jnp.float32).reshape(G, HG, 1)
    y = y.reshape(bsz, seq, G, HG * P) * jax.nn.silu(z.astype(jnp.float32).reshape(bsz, seq, G, HG * P))
    y = y * lax.rsqrt(jnp.mean(y * y, axis=-1, keepdims=True) + RMS_EPS)
    y = y * norm_w.astype(jnp.float32).reshape(G, HG * P)
    return y.reshape(bsz, seq, SSD_WIDTH).astype(z.dtype)


def dilated_branch(q, k, v, slopes, window, dilation):
    bsz, seq, H, E = q.shape
    L = seq // dilation
    C = ATT_BLOCK
    w_sub = window // dilation
    nb = -(-L // C)
    Lp = nb * C

    def to_sub(t):
        return t.reshape(bsz, L, dilation, H, E).transpose(0, 2, 3, 1, 4)

    qs = jnp.pad(to_sub(q), ((0, 0), (0, 0), (0, 0), (0, Lp - L), (0, 0))).reshape(bsz, dilation, H, nb, C, E)

    def key_blocks(t):
        t = jnp.pad(to_sub(t), ((0, 0), (0, 0), (0, 0), (C, Lp - L), (0, 0))).reshape(bsz, dilation, H, nb + 1, C, E)
        return jnp.concatenate([t[:, :, :, :-1], t[:, :, :, 1:]], axis=4)

    kb = key_blocks(k)
    vb = key_blocks(v)
    s = jnp.einsum('bdhnqe,bdhnke->bdhnqk', qs, kb)
    qi = jnp.arange(C)[:, None]
    kj = jnp.arange(2 * C)[None, :]
    dist = C + qi - kj
    key_pos = (jnp.arange(nb)[:, None, None] - 1) * C + kj[None]
    valid = (dist >= 0) & (dist <= w_sub) & (key_pos >= 0)
    bias = -slopes[:, None, None] * (dist * dilation).astype(jnp.float32)
    s = jnp.where(valid[None, None, None], s + bias[None, None, :, None], -jnp.inf)
    m = jnp.max(s, axis=-1, keepdims=True)
    pexp = jnp.exp(s - m)
    l = jnp.sum(pexp, axis=-1, keepdims=True)
    o = jnp.einsum('bdhnqk,bdhnke->bdhnqe', pexp, vb) / l
    lse = (m + jnp.log(l))[..., 0]
    o = o.reshape(bsz, dilation, H, Lp, E)[:, :, :, :L].transpose(0, 3, 1, 2, 4).reshape(bsz, seq, H, E)
    lse = lse.reshape(bsz, dilation, H, Lp)[:, :, :, :L].transpose(0, 3, 1, 2).reshape(bsz, seq, H)
    return o, lse


def dilated_attention(q, k, v):
    slopes = jnp.asarray(alibi_slopes(ATT_HEADS))
    outs = []
    lses = []
    for window, dilation in ATT_BRANCHES:
        o, lse = dilated_branch(q, k, v, slopes, window, dilation)
        outs.append(o)
        lses.append(lse)
    wts = jax.nn.softmax(jnp.stack(lses), axis=0)
    return jnp.einsum('gbsh,gbshe->bshe', wts, jnp.stack(outs))


def hybrid_mixer(u, w_in, conv_w, conv_b, dt_bias, a_log, d_skip, ssd_norm_w, w_out):
    bsz, seq, _ = u.shape
    proj = u @ w_in
    i1 = SSD_WIDTH
    i2 = i1 + SSD_CONV_DIM
    i3 = i2 + SSD_HEADS
    i4 = i3 + ATT_WIDTH
    i5 = i4 + ATT_WIDTH
    z, xbc, dt_raw, q, k, v = jnp.split(proj, [i1, i2, i3, i4, i5], axis=-1)
    ssd = ssd_mixer(z, xbc, dt_raw, conv_w, conv_b, dt_bias, a_log, d_skip, ssd_norm_w)
    q = q.astype(jnp.float32).reshape(bsz, seq, ATT_HEADS, ATT_HEAD_DIM) * (ATT_HEAD_DIM ** -0.5)
    k = k.astype(jnp.float32).reshape(bsz, seq, ATT_HEADS, ATT_HEAD_DIM)
    v = v.astype(jnp.float32).reshape(bsz, seq, ATT_HEADS, ATT_HEAD_DIM)
    att = dilated_attention(q, k, v).reshape(bsz, seq, ATT_WIDTH).astype(u.dtype)
    return jnp.concatenate([ssd, att], axis=-1) @ w_out


def setup_inputs(seed: int = 0) -> dict:
    key = jax.random.key(seed)
    ks = jax.random.split(key, 28)

    def nrm(k, shape, scale):
        return jax.random.normal(k, shape, jnp.float32) * scale

    def gain(k, n):
        return 1.0 + 0.02 * jax.random.normal(k, (DEPTH, n), jnp.float32)

    dt = jnp.exp(jax.random.uniform(ks[13], (DEPTH, SSD_HEADS), jnp.float32) * (math.log(0.1) - math.log(0.001)) + math.log(0.001))
    dt = jnp.maximum(dt, 1e-4)
    dt_bias = dt + jnp.log(-jnp.expm1(-dt))
    a_log = jnp.log(jax.random.uniform(ks[14], (DEPTH, SSD_HEADS), jnp.float32, 1.0, 16.0))
    return {
        'x': jax.random.normal(ks[0], (BATCH, SEQ, D_MODEL), jnp.float32),
        'p': jax.random.normal(ks[1], (DEPTH, BATCH, SEQ, PLE_DIM), jnp.float32),
        'ffn1_pre_w': gain(ks[2], D_MODEL),
        'ffn1_post_w': gain(ks[3], D_MODEL),
        'ffn1_w_gate': nrm(ks[4], (DEPTH, D_MODEL, D_FF), D_MODEL ** -0.5),
        'ffn1_w_up': nrm(ks[5], (DEPTH, D_MODEL, D_FF), D_MODEL ** -0.5),
        'ffn1_w_down': nrm(ks[6], (DEPTH, D_FF, D_MODEL), D_FF ** -0.5),
        'mix_pre_w': gain(ks[7], D_MODEL),
        'mix_post_w': gain(ks[8], D_MODEL),
        'w_in': nrm(ks[9], (DEPTH, D_MODEL, IN_PROJ_WIDTH), D_MODEL ** -0.5),
        'conv_w': nrm(ks[10], (DEPTH, SSD_CONV, SSD_CONV_DIM), SSD_CONV ** -0.5),
        'conv_b': nrm(ks[11], (DEPTH, SSD_CONV_DIM), 0.01),
        'dt_bias': dt_bias,
        'a_log': a_log,
        'd_skip': 1.0 + 0.1 * jax.random.normal(ks[15], (DEPTH, SSD_HEADS), jnp.float32),
        'ssd_norm_w': gain(ks[16], SSD_WIDTH),
        'w_out': nrm(ks[17], (DEPTH, MIX_WIDTH, D_MODEL), MIX_WIDTH ** -0.5),
        'ffn2_pre_w': gain(ks[18], D_MODEL),
        'ffn2_post_w': gain(ks[19], D_MODEL),
        'ffn2_w_gate': nrm(ks[20], (DEPTH, D_MODEL, D_FF), D_MODEL ** -0.5),
        'ffn2_w_up': nrm(ks[21], (DEPTH, D_MODEL, D_FF), D_MODEL ** -0.5),
        'ffn2_w_down': nrm(ks[22], (DEPTH, D_FF, D_MODEL), D_FF ** -0.5),
        'ple_pre_w': gain(ks[23], D_MODEL),
        'ple_post_w': gain(ks[24], D_MODEL),
        'w_ple_gate': nrm(ks[25], (DEPTH, D_MODEL, D_MODEL), D_MODEL ** -0.5),
        'w_ple_proj': nrm(ks[26], (DEPTH, PLE_DIM, D_MODEL), PLE_DIM ** -0.5),
    }


def reference(x, p, ffn1_pre_w, ffn1_post_w, ffn1_w_gate, ffn1_w_up, ffn1_w_down,
              mix_pre_w, mix_post_w, w_in, conv_w, conv_b, dt_bias, a_log, d_skip, ssd_norm_w, w_out,
              ffn2_pre_w, ffn2_post_w, ffn2_w_gate, ffn2_w_up, ffn2_w_down,
              ple_pre_w, ple_post_w, w_ple_gate, w_ple_proj):
    h = x
    for i in range(DEPTH):
        f = swiglu(rms_norm(h, ffn1_pre_w[i]), ffn1_w_gate[i], ffn1_w_up[i], ffn1_w_down[i])
        h = h + 0.5 * rms_norm(f, ffn1_post_w[i])
        mix = hybrid_mixer(rms_norm(h, mix_pre_w[i]), w_in[i], conv_w[i], conv_b[i], dt_bias[i],
                           a_log[i], d_skip[i], ssd_norm_w[i], w_out[i])
        h = h + rms_norm(mix, mix_post_w[i])
        f = swiglu(rms_norm(h, ffn2_pre_w[i]), ffn2_w_gate[i], ffn2_w_up[i], ffn2_w_down[i])
        h = h + 0.5 * rms_norm(f, ffn2_post_w[i])
        gate = jax.nn.sigmoid(rms_norm(h, ple_pre_w[i]) @ w_ple_gate[i])
        h = h + rms_norm(gate * (p[i].astype(h.dtype) @ w_ple_proj[i]), ple_post_w[i])
    return h
```

```python
import functools
import math

import numpy as np
import jax
import jax.numpy as jnp
from jax import lax
from jax.experimental import pallas as pl
from jax.experimental.pallas import tpu as pltpu

F32 = jnp.float32
BF16 = jnp.bfloat16

RMS_EPS = 1e-6

LANES = 128
SUBLANES = 8
VMEM_LIMIT_BYTES = 58 * 1024 * 1024

SSD_HEAD_DIM = 64
SSD_HEADS = 40
SSD_GROUPS = 8
SSD_HEADS_PER_GROUP = SSD_HEADS // SSD_GROUPS
SSD_GROUP_WIDTH = SSD_HEADS_PER_GROUP * SSD_HEAD_DIM
SSD_GROUP_PAD = 384
SSD_WIDTH = SSD_HEADS * SSD_HEAD_DIM
SSD_WIDTH_PAD = SSD_GROUPS * SSD_GROUP_PAD
SSD_STATE = 128
SSD_CONV = 4
SSD_CHUNK = 128
SSD_BC_WIDTH = 2 * SSD_GROUPS * SSD_STATE
CONV_HALO_ROWS = SUBLANES

ATT_HEAD_DIM = 128
ATT_HEADS = 12
ATT_WIDTH = ATT_HEADS * ATT_HEAD_DIM
ATT_BRANCHES = ((128, 1), (512, 4), (2048, 16))
ATT_BLOCK = 128

D_FF_PAD_MULTIPLE = 1024


def _cparams(semantics):
    return pltpu.CompilerParams(dimension_semantics=semantics, vmem_limit_bytes=VMEM_LIMIT_BYTES)


def _dot(a, b):
    return jnp.dot(a, b, preferred_element_type=F32)


def _dot_nt(a, b):
    return lax.dot_general(a, b, (((1,), (1,)), ((), ())), preferred_element_type=F32)


def _dot_tn(a, b):
    return lax.dot_general(a, b, (((0,), (0,)), ((), ())), preferred_element_type=F32)


def _split_bf16(x):
    hi = x.astype(BF16)
    lo = (x - hi.astype(F32)).astype(BF16)
    return hi, lo


def _dot_split_lhs(x, sel):
    hi, lo = _split_bf16(x)
    return _dot(hi, sel) + _dot(lo, sel)


def _silu(x):
    return x * jax.nn.sigmoid(x)


def _rmsnorm_kernel(x_ref, w_ref, o_ref, *, rows_per_step):
    n_steps = x_ref.shape[0] // rows_per_step

    def body(r, carry):
        rows = pl.ds(pl.multiple_of(r * rows_per_step, rows_per_step), rows_per_step)
        x = x_ref[rows, :]
        ms = jnp.mean(x * x, axis=-1, keepdims=True)
        o_ref[rows, :] = (x * lax.rsqrt(ms + RMS_EPS) * w_ref[...]).astype(o_ref.dtype)
        return carry

    lax.fori_loop(0, n_steps, body, 0)


def _rmsnorm(x, w, *, tm=256):
    m, d = x.shape
    return pl.pallas_call(
        functools.partial(_rmsnorm_kernel, rows_per_step=16),
        out_shape=jax.ShapeDtypeStruct((m, d), BF16),
        grid=(m // tm,),
        in_specs=[pl.BlockSpec((tm, d), lambda i: (i, 0)),
                  pl.BlockSpec((1, d), lambda i: (0, 0))],
        out_specs=pl.BlockSpec((tm, d), lambda i: (i, 0)),
        compiler_params=_cparams(("parallel",)),
        name="rmsnorm",
    )(x, w.reshape(1, d))


def _matmul_kernel(a_ref, w_ref, o_ref):
    o_ref[...] = _dot(a_ref[...], w_ref[...]).astype(o_ref.dtype)


def _matmul(a, w, out_dtype, *, tm=1024, tn=512, name="matmul"):
    m, k = a.shape
    n = w.shape[1]
    tn = min(tn, n)
    return pl.pallas_call(
        _matmul_kernel,
        out_shape=jax.ShapeDtypeStruct((m, n), out_dtype),
        grid=(m // tm, n // tn),
        in_specs=[pl.BlockSpec((tm, k), lambda i, j: (i, 0)),
                  pl.BlockSpec((k, tn), lambda i, j: (0, j))],
        out_specs=pl.BlockSpec((tm, tn), lambda i, j: (i, j)),
        compiler_params=_cparams(("parallel", "arbitrary")),
        name=name,
    )(a, w)


def _gateup_kernel(a_ref, wg_ref, wu_ref, o_ref):
    a = a_ref[...]
    g = _dot(a, wg_ref[...])
    u = _dot(a, wu_ref[...])
    o_ref[...] = (_silu(g) * u).astype(o_ref.dtype)


def _gateup(a, wg, wu, *, tm=1024, tn=512):
    m, k = a.shape
    n = wg.shape[1]
    return pl.pallas_call(
        _gateup_kernel,
        out_shape=jax.ShapeDtypeStruct((m, n), BF16),
        grid=(m // tm, n // tn),
        in_specs=[pl.BlockSpec((tm, k), lambda i, j: (i, 0)),
                  pl.BlockSpec((k, tn), lambda i, j: (0, j)),
                  pl.BlockSpec((k, tn), lambda i, j: (0, j))],
        out_specs=pl.BlockSpec((tm, tn), lambda i, j: (i, j)),
        compiler_params=_cparams(("parallel", "arbitrary")),
        name="swiglu_gateup",
    )(a, wg, wu)


def _postnorm_rows(o_ref, h_ref, nw_ref, scale, rows_per_step, transform=None):
    n_steps = o_ref.shape[0] // rows_per_step

    def body(r, carry):
        rows = pl.ds(pl.multiple_of(r * rows_per_step, rows_per_step), rows_per_step)
        f = o_ref[rows, :]
        if transform is not None:
            f = transform(f, rows)
        ms = jnp.mean(f * f, axis=-1, keepdims=True)
        y = f * lax.rsqrt(ms + RMS_EPS) * nw_ref[...]
        o_ref[rows, :] = h_ref[rows, :] + scale * y
        return carry

    lax.fori_loop(0, n_steps, body, 0)


def _accumulate(o_ref, part, k):
    @pl.when(k == 0)
    def _():
        o_ref[...] = part

    @pl.when(k > 0)
    def _():
        o_ref[...] += part


def _matmul_postnorm_kernel(a_ref, w_ref, h_ref, nw_ref, o_ref, *, scale):
    k = pl.program_id(1)
    _accumulate(o_ref, _dot(a_ref[...], w_ref[...]), k)

    @pl.when(k == pl.num_programs(1) - 1)
    def _():
        _postnorm_rows(o_ref, h_ref, nw_ref, scale, 16)


def _matmul_postnorm(a, w, h, nw, scale, *, tm=512, tk=512, name="matmul_postnorm"):
    m, kdim = a.shape
    n = w.shape[1]
    return pl.pallas_call(
        functools.partial(_matmul_postnorm_kernel, scale=scale),
        out_shape=jax.ShapeDtypeStruct((m, n), F32),
        grid=(m // tm, kdim // tk),
        in_specs=[pl.BlockSpec((tm, tk), lambda i, k: (i, k)),
                  pl.BlockSpec((tk, n), lambda i, k: (k, 0)),
                  pl.BlockSpec((tm, n), lambda i, k: (i, 0)),
                  pl.BlockSpec((1, n), lambda i, k: (0, 0))],
        out_specs=pl.BlockSpec((tm, n), lambda i, k: (i, 0)),
        compiler_params=_cparams(("parallel", "arbitrary")),
        name=name,
    )(a, w, h, nw.reshape(1, n))


def _ple_kernel(a_ref, w_ref, h_ref, nw_ref, p_ref, wp_ref, o_ref):
    k = pl.program_id(1)
    _accumulate(o_ref, _dot(a_ref[...], w_ref[...]), k)

    @pl.when(k == pl.num_programs(1) - 1)
    def _():
        def gate_times_proj(f, rows):
            proj = _dot(p_ref[rows, :].astype(BF16), wp_ref[...])
            return jax.nn.sigmoid(f) * proj

        _postnorm_rows(o_ref, h_ref, nw_ref, 1.0, 64, transform=gate_times_proj)


def _ple(a, w, h, nw, p, wp, *, tm=512, tk=512):
    m, kdim = a.shape
    n = w.shape[1]
    pdim = p.shape[1]
    return pl.pallas_call(
        _ple_kernel,
        out_shape=jax.ShapeDtypeStruct((m, n), F32),
        grid=(m // tm, kdim // tk),
        in_specs=[pl.BlockSpec((tm, tk), lambda i, k: (i, k)),
                  pl.BlockSpec((tk, n), lambda i, k: (k, 0)),
                  pl.BlockSpec((tm, n), lambda i, k: (i, 0)),
                  pl.BlockSpec((1, n), lambda i, k: (0, 0)),
                  pl.BlockSpec((tm, pdim), lambda i, k: (i, 0)),
                  pl.BlockSpec((pdim, n), lambda i, k: (0, 0))],
        out_specs=pl.BlockSpec((tm, n), lambda i, k: (i, 0)),
        compiler_params=_cparams(("parallel", "arbitrary")),
        name="ple_gate",
    )(a, w, h, nw.reshape(1, n), p, wp)


def _ssd_kernel(z_ref, x_ref, bc_ref, dt_ref, cwx_ref, cbx_ref, cwbc_ref, cbbc_ref, dtb_ref, alog_ref,
                dskip_ref, nw_ref, head_to_feat_ref, head_to_tile_ref, o_ref, xbuf, bcbuf, state):
    c = pl.program_id(1)
    q = SSD_CHUNK
    halo = CONV_HALO_ROWS

    @pl.when(c == 0)
    def _():
        xbuf[0:halo, :] = jnp.zeros((halo, xbuf.shape[1]), F32)
        bcbuf[0:halo, :] = jnp.zeros((halo, bcbuf.shape[1]), F32)
        state[...] = jnp.zeros(state.shape, F32)

    xbuf[halo:halo + q, :] = x_ref[...].astype(F32)
    bcbuf[halo:halo + q, :] = bc_ref[...].astype(F32)

    v = dt_ref[...] + dtb_ref[...]
    dt = jnp.maximum(v, 0.0) + jnp.log1p(jnp.exp(-jnp.abs(v)))
    da = dt * -jnp.exp(alog_ref[...])
    row = lax.broadcasted_iota(jnp.int32, (q, q), 0)
    col = lax.broadcasted_iota(jnp.int32, (q, q), 1)
    causal = row >= col
    tril = jnp.where(causal, 1.0, 0.0).astype(BF16)
    da_hi, da_lo = _split_bf16(da)
    a_cs = _dot(tril, da_hi) + _dot(tril, da_lo)
    a_cs_t = a_cs.T
    head_to_feat = head_to_feat_ref[...]
    a_feat = _dot_split_lhs(a_cs, head_to_feat)
    dt_feat = _dot_split_lhs(dt, head_to_feat)
    a_tile = _dot_split_lhs(a_cs, head_to_tile_ref[...])

    lane = lax.broadcasted_iota(jnp.int32, (q, SSD_GROUP_PAD), 1)

    for g in range(SSD_GROUPS):
        lo, hi = g * SSD_GROUP_PAD, (g + 1) * SSD_GROUP_PAD
        blo, clo = g * SSD_STATE, (SSD_GROUPS + g) * SSD_STATE

        def conv(buf, w_ref, b_ref, c0, c1):
            acc = b_ref[:, c0:c1] + w_ref[0:1, c0:c1] * buf[pl.ds(halo - 3, q), c0:c1]
            for t in range(1, SSD_CONV):
                acc = acc + w_ref[t:t + 1, c0:c1] * buf[pl.ds(halo - 3 + t, q), c0:c1]
            return _silu(acc)

        xg = conv(xbuf, cwx_ref, cbx_ref, lo, hi)
        bg = conv(bcbuf, cwbc_ref, cbbc_ref, blo, blo + SSD_STATE)
        cg = conv(bcbuf, cwbc_ref, cbbc_ref, clo, clo + SSD_STATE)
        bg16 = bg.astype(BF16)
        cg16 = cg.astype(BF16)

        a_g = a_feat[:, lo:hi]
        a_last = a_g[q - 1:q, :]
        xdt = xg * dt_feat[:, lo:hi]

        cb = _dot_nt(cg16, bg16)
        m_parts = []
        x_parts = []
        for jj in range(SSD_HEADS_PER_GROUP):
            j = g * SSD_HEADS_PER_GROUP + jj
            seg = a_tile[:, j * q:(j + 1) * q] - a_cs_t[j:j + 1, :]
            decay = jnp.exp(jnp.where(causal, seg, -jnp.inf))
            m_parts.append((cb * decay).astype(BF16))
            in_head = (lane >= jj * SSD_HEAD_DIM) & (lane < (jj + 1) * SSD_HEAD_DIM)
            x_parts.append(jnp.where(in_head, xdt, 0.0).astype(BF16))
        y = _dot(jnp.concatenate(m_parts, axis=1), jnp.concatenate(x_parts, axis=0))

        prev = state[:, lo:hi]
        y = y + _dot(cg16, prev.astype(BF16)) * jnp.exp(a_g)
        xdd = (xdt * jnp.exp(a_last - a_g)).astype(BF16)
        state[:, lo:hi] = prev * jnp.exp(a_last) + _dot_tn(bg16, xdd)

        y = y + xg * dskip_ref[:, lo:hi]
        y = y * _silu(z_ref[:, lo:hi].astype(F32))
        ms = jnp.sum(y * y, axis=-1, keepdims=True) * (1.0 / SSD_GROUP_WIDTH)
        o_ref[:, lo:hi] = (y * lax.rsqrt(ms + RMS_EPS) * nw_ref[:, lo:hi]).astype(o_ref.dtype)

    xbuf[0:halo, :] = xbuf[q:q + halo, :]
    bcbuf[0:halo, :] = bcbuf[q:q + halo, :]


def _ssd(zx, bc, dt, consts, *, batch, seq):
    (cwx, cbx, cwbc, cbbc, dtb, alog, dskip, nw, head_to_feat, head_to_tile) = consts
    q = SSD_CHUNK
    nc = seq // q
    m = batch * seq
    wp = SSD_WIDTH_PAD

    def tok(b, c):
        return b * nc + c

    def const_spec(arr):
        return pl.BlockSpec(arr.shape, lambda b, c: (0, 0))

    return pl.pallas_call(
        _ssd_kernel,
        out_shape=jax.ShapeDtypeStruct((m, wp), BF16),
        grid=(batch, nc),
        in_specs=[pl.BlockSpec((q, wp), lambda b, c: (tok(b, c), 0)),
                  pl.BlockSpec((q, wp), lambda b, c: (tok(b, c), 1)),
                  pl.BlockSpec((q, SSD_BC_WIDTH), lambda b, c: (tok(b, c), 0)),
                  pl.BlockSpec((q, LANES), lambda b, c: (tok(b, c), 0)),
                  const_spec(cwx), const_spec(cbx), const_spec(cwbc), const_spec(cbbc),
                  const_spec(dtb), const_spec(alog), const_spec(dskip), const_spec(nw),
                  const_spec(head_to_feat), const_spec(head_to_tile)],
        out_specs=pl.BlockSpec((q, wp), lambda b, c: (tok(b, c), 0)),
        scratch_shapes=[pltpu.VMEM((q + CONV_HALO_ROWS, wp), F32),
                        pltpu.VMEM((q + CONV_HALO_ROWS, SSD_BC_WIDTH), F32),
                        pltpu.VMEM((SSD_STATE, wp), F32)],
        compiler_params=_cparams(("parallel", "arbitrary")),
        name="ssd_chunk_scan",
    )(zx, zx, bc, dt, cwx, cbx, cwbc, cbbc, dtb, alog, dskip, nw, head_to_feat, head_to_tile)


def _alibi_slopes(n):
    def pow2(m):
        start = 2.0 ** (-8.0 / m)
        return [start ** (i + 1) for i in range(m)]
    if math.log2(n).is_integer():
        s = pow2(n)
    else:
        c = 2 ** int(math.floor(math.log2(n)))
        s = pow2(c) + pow2(2 * c)[0::2][: n - c]
    return [float(np.float32(v)) for v in s]


def _attn_branch_kernel(q_ref, k_ref, kp_ref, v_ref, vp_ref, o_ref, lse_ref, *, dilation, slopes):
    n = pl.program_id(2)
    c = ATT_BLOCK
    qi = lax.broadcasted_iota(jnp.int32, (c, c), 0)
    kj = lax.broadcasted_iota(jnp.int32, (c, c), 1)
    dist_cur = (qi - kj).astype(F32)
    dist_prev = dist_cur + float(c)
    valid_cur = kj <= qi
    valid_prev = (kj >= qi) & (n > 0)
    lane = lax.broadcasted_iota(jnp.int32, (c, LANES), 1)
    lse_tile = jnp.zeros((c, LANES), F32)
    scale = ATT_HEAD_DIM ** -0.5

    for h in range(ATT_HEADS):
        sl = slice(h * ATT_HEAD_DIM, (h + 1) * ATT_HEAD_DIM)
        qh = q_ref[0, :, sl]
        tok_slope = slopes[h] * dilation
        s_cur = _dot_nt(qh, k_ref[0, :, sl]) * scale - tok_slope * dist_cur
        s_prev = _dot_nt(qh, kp_ref[0, :, sl]) * scale - tok_slope * dist_prev
        s_cur = jnp.where(valid_cur, s_cur, -jnp.inf)
        s_prev = jnp.where(valid_prev, s_prev, -jnp.inf)
        mx = jnp.maximum(jnp.max(s_cur, axis=-1, keepdims=True), jnp.max(s_prev, axis=-1, keepdims=True))
        p_cur = jnp.exp(s_cur - mx)
        p_prev = jnp.exp(s_prev - mx)
        den = jnp.sum(p_cur, axis=-1, keepdims=True) + jnp.sum(p_prev, axis=-1, keepdims=True)
        o = _dot(p_cur.astype(BF16), v_ref[0, :, sl]) + _dot(p_prev.astype(BF16), vp_ref[0, :, sl])
        o_ref[0, :, sl] = (o / den).astype(o_ref.dtype)
        lse_tile = jnp.where(lane == h, mx + jnp.log(den), lse_tile)

    lse_ref[0] = lse_tile


def _attn_branch(qkv, dilation, *, batch, seq):
    d = dilation
    sub_len = seq // d
    nb = sub_len // ATT_BLOCK
    w = ATT_WIDTH
    qkv_v = qkv.reshape(batch, sub_len, d * 3 * w)
    blk = (1, ATT_BLOCK, w)
    o, lse = pl.pallas_call(
        functools.partial(_attn_branch_kernel, dilation=d, slopes=_alibi_slopes(ATT_HEADS)),
        out_shape=(jax.ShapeDtypeStruct((batch, sub_len, d * w), BF16),
                   jax.ShapeDtypeStruct((batch, sub_len, d * LANES), F32)),
        grid=(batch, d, nb),
        in_specs=[pl.BlockSpec(blk, lambda b, r, n: (b, n, 3 * r)),
                  pl.BlockSpec(blk, lambda b, r, n: (b, n, 3 * r + 1)),
                  pl.BlockSpec(blk, lambda b, r, n: (b, jnp.maximum(n - 1, 0), 3 * r + 1)),
                  pl.BlockSpec(blk, lambda b, r, n: (b, n, 3 * r + 2)),
                  pl.BlockSpec(blk, lambda b, r, n: (b, jnp.maximum(n - 1, 0), 3 * r + 2))],
        out_specs=(pl.BlockSpec(blk, lambda b, r, n: (b, n, r)),
                   pl.BlockSpec((1, ATT_BLOCK, LANES), lambda b, r, n: (b, n, r))),
        compiler_params=_cparams(("parallel", "parallel", "arbitrary")),
        name=f"dilated_attn_d{d}",
    )(qkv_v, qkv_v, qkv_v, qkv_v, qkv_v)
    return o.reshape(batch * seq, w), lse.reshape(batch * seq, LANES)


def _attn_merge_kernel(o1_ref, o2_ref, o3_ref, l1_ref, l2_ref, l3_ref, head_to_feat_ref, o_ref):
    lses = [l1_ref[...], l2_ref[...], l3_ref[...]]
    mx = jnp.maximum(jnp.maximum(lses[0], lses[1]), lses[2])
    es = [jnp.exp(l - mx) for l in lses]
    inv = 1.0 / (es[0] + es[1] + es[2])
    sel = head_to_feat_ref[...]
    acc = None
    for e, oref in zip(es, (o1_ref, o2_ref, o3_ref)):
        term = _dot_split_lhs(e * inv, sel) * oref[...].astype(F32)
        acc = term if acc is None else acc + term
    o_ref[...] = acc.astype(o_ref.dtype)


def _attn_merge(outs, lses, head_to_feat, *, tm=256):
    m, w = outs[0].shape
    ospec = pl.BlockSpec((tm, w), lambda i: (i, 0))
    lspec = pl.BlockSpec((tm, LANES), lambda i: (i, 0))
    return pl.pallas_call(
        _attn_merge_kernel,
        out_shape=jax.ShapeDtypeStruct((m, w), BF16),
        grid=(m // tm,),
        in_specs=[ospec, ospec, ospec, lspec, lspec, lspec,
                  pl.BlockSpec(head_to_feat.shape, lambda i: (0, 0))],
        out_specs=ospec,
        compiler_params=_cparams(("parallel",)),
        name="attn_merge",
    )(*outs, *lses, head_to_feat)


def _pad_groups(t):
    lead = t.shape[:-1]
    t = t.reshape(*lead, SSD_GROUPS, SSD_GROUP_WIDTH)
    t = jnp.pad(t, [(0, 0)] * len(lead) + [(0, 0), (0, SSD_GROUP_PAD - SSD_GROUP_WIDTH)])
    return t.reshape(*lead, SSD_WIDTH_PAD)


def _pad_last(t, n):
    return jnp.pad(t, [(0, 0)] * (t.ndim - 1) + [(0, n - t.shape[-1])])


def _head_selectors():
    head_to_feat = np.zeros((LANES, SSD_WIDTH_PAD), np.float32)
    head_to_tile = np.zeros((LANES, SSD_HEADS * SSD_CHUNK), np.float32)
    for j in range(SSD_HEADS):
        g, jj = divmod(j, SSD_HEADS_PER_GROUP)
        f0 = g * SSD_GROUP_PAD + jj * SSD_HEAD_DIM
        head_to_feat[j, f0:f0 + SSD_HEAD_DIM] = 1.0
        head_to_tile[j, j * SSD_CHUNK:(j + 1) * SSD_CHUNK] = 1.0
    att_head_to_feat = np.zeros((LANES, ATT_WIDTH), np.float32)
    for h in range(ATT_HEADS):
        att_head_to_feat[h, h * ATT_HEAD_DIM:(h + 1) * ATT_HEAD_DIM] = 1.0
    return (jnp.asarray(head_to_feat, BF16), jnp.asarray(head_to_tile, BF16), jnp.asarray(att_head_to_feat, BF16))


def _ffn(h, pre_w, post_w, w_gate, w_up, w_down):
    d, f = w_gate.shape
    f_pad = -(-f // D_FF_PAD_MULTIPLE) * D_FF_PAD_MULTIPLE
    wg = _pad_last(w_gate, f_pad).astype(BF16)
    wu = _pad_last(w_up, f_pad).astype(BF16)
    wd = jnp.pad(w_down, ((0, f_pad - f), (0, 0))).astype(BF16)
    xn = _rmsnorm(h, pre_w)
    act = _gateup(xn, wg, wu)
    return _matmul_postnorm(act, wd, h, post_w, 0.5, name="ffn_down_postnorm")


def kernel(x, p, ffn1_pre_w, ffn1_post_w, ffn1_w_gate, ffn1_w_up, ffn1_w_down, mix_pre_w, mix_post_w, w_in, conv_w, conv_b, dt_bias, a_log, d_skip, ssd_norm_w, w_out, ffn2_pre_w, ffn2_post_w, ffn2_w_gate, ffn2_w_up, ffn2_w_down, ple_pre_w, ple_post_w, w_ple_gate, w_ple_proj):
    batch, seq, d_model = x.shape
    depth = w_in.shape[0]
    m = batch * seq
    h = x.reshape(m, d_model)
    head_to_feat, head_to_tile, att_head_to_feat = _head_selectors()
    i1 = SSD_WIDTH
    i2 = i1 + SSD_WIDTH
    i3 = i2 + SSD_BC_WIDTH
    i4 = i3 + SSD_HEADS

    for i in range(depth):
        h = _ffn(h, ffn1_pre_w[i], ffn1_post_w[i], ffn1_w_gate[i], ffn1_w_up[i], ffn1_w_down[i])

        wi = w_in[i]
        w_zx = jnp.concatenate([_pad_groups(wi[:, :i1]), _pad_groups(wi[:, i1:i2])], axis=1).astype(BF16)
        w_bc = wi[:, i2:i3].astype(BF16)
        w_dt = _pad_last(wi[:, i3:i4], LANES).astype(BF16)
        w_qkv = wi[:, i4:].astype(BF16)
        u = _rmsnorm(h, mix_pre_w[i])
        zx = _matmul(u, w_zx, BF16, name="in_proj_zx")
        bc = _matmul(u, w_bc, BF16, name="in_proj_bc")
        dt = _matmul(u, w_dt, F32, name="in_proj_dt")
        qkv = _matmul(u, w_qkv, BF16, name="in_proj_qkv")

        cw, cb = conv_w[i], conv_b[i]
        ssd_consts = (
            _pad_groups(cw[:, :SSD_WIDTH]), _pad_groups(cb[None, :SSD_WIDTH]),
            cw[:, SSD_WIDTH:], cb[None, SSD_WIDTH:],
            _pad_last(dt_bias[i][None, :], LANES),
            _pad_last(a_log[i][None, :], LANES),
            _pad_groups(jnp.repeat(d_skip[i], SSD_HEAD_DIM)[None, :]),
            _pad_groups(ssd_norm_w[i][None, :]),
            head_to_feat, head_to_tile)
        y_ssd = _ssd(zx, bc, dt, ssd_consts, batch=batch, seq=seq)

        outs, lses = [], []
        for _, dilation in ATT_BRANCHES:
            o, lse = _attn_branch(qkv, dilation, batch=batch, seq=seq)
            outs.append(o)
            lses.append(lse)
        att = _attn_merge(outs, lses, att_head_to_feat)

        wo = w_out[i]
        wo_ssd = jnp.pad(wo[:SSD_WIDTH].reshape(SSD_GROUPS, SSD_GROUP_WIDTH, d_model),
                         ((0, 0), (0, SSD_GROUP_PAD - SSD_GROUP_WIDTH), (0, 0))).reshape(SSD_WIDTH_PAD, d_model)
        wo_pad = jnp.concatenate([wo_ssd, wo[SSD_WIDTH:]], axis=0).astype(BF16)
        mixed = jnp.concatenate([y_ssd, att], axis=1)
        h = _matmul_postnorm(mixed, wo_pad, h, mix_post_w[i], 1.0, name="out_proj_postnorm")

        h = _ffn(h, ffn2_pre_w[i], ffn2_post_w[i], ffn2_w_gate[i], ffn2_w_up[i], ffn2_w_down[i])

        xn = _rmsnorm(h, ple_pre_w[i])
        h = _ple(xn, w_ple_gate[i].astype(BF16), h, ple_post_w[i], p[i].reshape(m, -1), w_ple_proj[i].astype(BF16))

    return h.reshape(batch, seq, d_model)
```

```python
import functools
import math

import numpy as np
import jax
import jax.numpy as jnp
from jax import lax
from jax.experimental import pallas as pl
from jax.experimental.pallas import tpu as pltpu

F32 = jnp.float32
BF16 = jnp.bfloat16

RMS_EPS = 1e-6

LANES = 128
SUBLANES = 8
VMEM_LIMIT_BYTES = 58 * 1024 * 1024

SSD_HEAD_DIM = 64
SSD_HEADS = 40
SSD_GROUPS = 8
SSD_HEADS_PER_GROUP = SSD_HEADS // SSD_GROUPS
SSD_GROUP_WIDTH = SSD_HEADS_PER_GROUP * SSD_HEAD_DIM
SSD_GROUP_PAD = 384
SSD_WIDTH = SSD_HEADS * SSD_HEAD_DIM
SSD_WIDTH_PAD = SSD_GROUPS * SSD_GROUP_PAD
SSD_STATE = 128
SSD_CONV = 4
SSD_CHUNK = 128
SSD_BC_WIDTH = 2 * SSD_GROUPS * SSD_STATE
CONV_HALO_ROWS = SUBLANES

ATT_HEAD_DIM = 128
ATT_HEADS = 12
ATT_WIDTH = ATT_HEADS * ATT_HEAD_DIM
ATT_BRANCHES = ((128, 1), (512, 4), (2048, 16))
ATT_BLOCK = 128


def _cparams(semantics):
    return pltpu.CompilerParams(dimension_semantics=semantics, vmem_limit_bytes=VMEM_LIMIT_BYTES)


def _dot(a, b):
    return jnp.dot(a, b, preferred_element_type=F32)


def _dot_nt(a, b):
    return lax.dot_general(a, b, (((1,), (1,)), ((), ())), preferred_element_type=F32)


def _dot_tn(a, b):
    return lax.dot_general(a, b, (((0,), (0,)), ((), ())), preferred_element_type=F32)


def _split_bf16(x):
    hi = x.astype(BF16)
    lo = (x - hi.astype(F32)).astype(BF16)
    return hi, lo


def _dot_split_lhs(x, sel):
    hi, lo = _split_bf16(x)
    return _dot(hi, sel) + _dot(lo, sel)


def _silu(x):
    return x * jax.nn.sigmoid(x)


def _rmsnorm_kernel(x_ref, w_ref, o_ref, *, rows_per_step):
    n_steps = x_ref.shape[0] // rows_per_step

    def body(r, carry):
        rows = pl.ds(pl.multiple_of(r * rows_per_step, rows_per_step), rows_per_step)
        x = x_ref[rows, :]
        ms = jnp.mean(x * x, axis=-1, keepdims=True)
        o_ref[rows, :] = (x * lax.rsqrt(ms + RMS_EPS) * w_ref[...]).astype(o_ref.dtype)
        return carry

    lax.fori_loop(0, n_steps, body, 0)


def _rmsnorm(x, w, *, tm=256):
    m, d = x.shape
    return pl.pallas_call(
        functools.partial(_rmsnorm_kernel, rows_per_step=16),
        out_shape=jax.ShapeDtypeStruct((m, d), BF16),
        grid=(m // tm,),
        in_specs=[pl.BlockSpec((tm, d), lambda i: (i, 0)),
                  pl.BlockSpec((1, d), lambda i: (0, 0))],
        out_specs=pl.BlockSpec((tm, d), lambda i: (i, 0)),
        compiler_params=_cparams(("parallel",)),
        name="rmsnorm",
    )(x, w.reshape(1, d))


def _matmul_kernel(a_ref, w_ref, o_ref):
    o_ref[...] = _dot(a_ref[...], w_ref[...]).astype(o_ref.dtype)


def _matmul(a, w, out_dtype, *, tm=1024, tn=512, name="matmul"):
    m, k = a.shape
    n = w.shape[1]
    tn = min(tn, n)
    return pl.pallas_call(
        _matmul_kernel,
        out_shape=jax.ShapeDtypeStruct((m, n), out_dtype),
        grid=(m // tm, n // tn),
        in_specs=[pl.BlockSpec((tm, k), lambda i, j: (i, 0)),
                  pl.BlockSpec((k, tn), lambda i, j: (0, j))],
        out_specs=pl.BlockSpec((tm, tn), lambda i, j: (i, j)),
        compiler_params=_cparams(("parallel", "arbitrary")),
        name=name,
    )(a, w)


def _gateup_kernel(a_ref, wg_ref, wu_ref, o_ref):
    a = a_ref[...]
    g = _dot(a, wg_ref[...].astype(BF16))
    u = _dot(a, wu_ref[...].astype(BF16))
    o_ref[...] = (_silu(g) * u).astype(o_ref.dtype)


def _gateup(a, wg, wu, *, tm=2048, tn=256):
    m, k = a.shape
    n = wg.shape[1]
    return pl.pallas_call(
        _gateup_kernel,
        out_shape=jax.ShapeDtypeStruct((m, n), BF16),
        grid=(m // tm, n // tn),
        in_specs=[pl.BlockSpec((tm, k), lambda i, j: (i, 0), pipeline_mode=pl.Buffered(1)),
                  pl.BlockSpec((k, tn), lambda i, j: (0, j)),
                  pl.BlockSpec((k, tn), lambda i, j: (0, j))],
        out_specs=pl.BlockSpec((tm, tn), lambda i, j: (i, j)),
        compiler_params=_cparams(("parallel", "arbitrary")),
        name="swiglu_gateup",
    )(a, wg, wu)


def _rms_scale(v):
    return lax.rsqrt(jnp.mean(v * v, axis=-1, keepdims=True) + RMS_EPS)


def _postnorm_rows(o_ref, h_ref, nw_ref, scale, rows_per_step, transform=None, next_w_ref=None, xn_ref=None,
                   unroll=1):
    n_steps = o_ref.shape[0] // rows_per_step

    def body(r, carry):
        rows = pl.ds(pl.multiple_of(r * rows_per_step, rows_per_step), rows_per_step)
        f = o_ref[rows, :]
        if transform is not None:
            f = transform(f, rows)
        hn = h_ref[rows, :] + scale * (f * _rms_scale(f) * nw_ref[...])
        o_ref[rows, :] = hn
        if xn_ref is not None:
            xn_ref[rows, :] = (hn * _rms_scale(hn) * next_w_ref[...]).astype(xn_ref.dtype)
        return carry

    lax.fori_loop(0, n_steps, body, 0, unroll=unroll)


def _store_column_tile(o_ref, part, j, tn):
    o_ref[:, pl.ds(pl.multiple_of(j * tn, tn), tn)] = part


def _matmul_postnorm_kernel(a_ref, w_ref, h_ref, nw_ref, next_w_ref, o_ref, xn_ref, *, scale, tn):
    j = pl.program_id(1)
    _store_column_tile(o_ref, _dot(a_ref[...], w_ref[...]), j, tn)

    @pl.when(j == pl.num_programs(1) - 1)
    def _():
        _postnorm_rows(o_ref, h_ref, nw_ref, scale, 16, next_w_ref=next_w_ref, xn_ref=xn_ref, unroll=4)


def _matmul_postnorm(a, w, h, nw, scale, next_w, *, tm=512, tn=256, name="matmul_postnorm"):
    m, kdim = a.shape
    n = w.shape[1]
    once = dict(pipeline_mode=pl.Buffered(1))
    row_spec = pl.BlockSpec((tm, n), lambda i, j: (i, 0))
    vec_spec = pl.BlockSpec((1, n), lambda i, j: (0, 0))
    return pl.pallas_call(
        functools.partial(_matmul_postnorm_kernel, scale=scale, tn=tn),
        out_shape=(jax.ShapeDtypeStruct((m, n), F32), jax.ShapeDtypeStruct((m, n), BF16)),
        grid=(m // tm, n // tn),
        in_specs=[pl.BlockSpec((tm, kdim), lambda i, j: (i, 0), **once),
                  pl.BlockSpec((kdim, tn), lambda i, j: (0, j)),
                  pl.BlockSpec((tm, n), lambda i, j: (i, 0), **once),
                  vec_spec, vec_spec],
        out_specs=(row_spec, row_spec),
        compiler_params=_cparams(("parallel", "arbitrary")),
        name=name,
    )(a, w, h, nw.reshape(1, n), next_w.reshape(1, n))


def _ple_kernel(a_ref, w_ref, h_ref, nw_ref, p_ref, wp_ref, o_ref, *, tn):
    j = pl.program_id(1)
    _store_column_tile(o_ref, _dot(a_ref[...], w_ref[...]), j, tn)

    @pl.when(j == pl.num_programs(1) - 1)
    def _():
        def gate_times_proj(f, rows):
            proj = _dot(p_ref[rows, :].astype(BF16), wp_ref[...])
            return jax.nn.sigmoid(f) * proj

        _postnorm_rows(o_ref, h_ref, nw_ref, 1.0, 64, transform=gate_times_proj)


def _ple(a, w, h, nw, p, wp, *, tm=512, tn=512):
    m, kdim = a.shape
    n = w.shape[1]
    pdim = p.shape[1]
    once = dict(pipeline_mode=pl.Buffered(1))
    return pl.pallas_call(
        functools.partial(_ple_kernel, tn=tn),
        out_shape=jax.ShapeDtypeStruct((m, n), F32),
        grid=(m // tm, n // tn),
        in_specs=[pl.BlockSpec((tm, kdim), lambda i, j: (i, 0), **once),
                  pl.BlockSpec((kdim, tn), lambda i, j: (0, j)),
                  pl.BlockSpec((tm, n), lambda i, j: (i, 0), **once),
                  pl.BlockSpec((1, n), lambda i, j: (0, 0)),
                  pl.BlockSpec((tm, pdim), lambda i, j: (i, 0)),
                  pl.BlockSpec((pdim, n), lambda i, j: (0, 0))],
        out_specs=pl.BlockSpec((tm, n), lambda i, j: (i, 0)),
        compiler_params=_cparams(("parallel", "arbitrary")),
        name="ple_gate",
    )(a, w, h, nw.reshape(1, n), p, wp)


def _ssd_kernel(z_ref, x_ref, bc_ref, dt_ref, cwx_ref, cbx_ref, cwbc_ref, cbbc_ref, dtb_ref, alog_ref,
                dskip_ref, nw_ref, head_to_feat_ref, head_to_tile_ref, o_ref, xbuf, bcbuf, state):
    c = pl.program_id(1)
    q = SSD_CHUNK
    halo = CONV_HALO_ROWS

    @pl.when(c == 0)
    def _():
        xbuf[0:halo, :] = jnp.zeros((halo, xbuf.shape[1]), F32)
        bcbuf[0:halo, :] = jnp.zeros((halo, bcbuf.shape[1]), F32)
        state[...] = jnp.zeros(state.shape, F32)

    xbuf[halo:halo + q, :] = x_ref[...].astype(F32)
    bcbuf[halo:halo + q, :] = bc_ref[...].astype(F32)

    v = dt_ref[...] + dtb_ref[...]
    dt = jnp.maximum(v, 0.0) + jnp.log1p(jnp.exp(-jnp.abs(v)))
    da = dt * -jnp.exp(alog_ref[...])
    row = lax.broadcasted_iota(jnp.int32, (q, q), 0)
    col = lax.broadcasted_iota(jnp.int32, (q, q), 1)
    causal = row >= col
    tril = jnp.where(causal, 1.0, 0.0).astype(BF16)
    da_hi, da_lo = _split_bf16(da)
    a_cs = _dot(tril, da_hi) + _dot(tril, da_lo)
    a_cs_t = a_cs.T
    head_to_feat = head_to_feat_ref[...]
    a_feat = _dot_split_lhs(a_cs, head_to_feat)
    dt_feat = _dot_split_lhs(dt, head_to_feat)
    a_tile = _dot_split_lhs(a_cs, head_to_tile_ref[...])

    lane = lax.broadcasted_iota(jnp.int32, (q, SSD_GROUP_PAD), 1)

    for g in range(SSD_GROUPS):
        lo, hi = g * SSD_GROUP_PAD, (g + 1) * SSD_GROUP_PAD
        blo, clo = g * SSD_STATE, (SSD_GROUPS + g) * SSD_STATE

        def conv(buf, w_ref, b_ref, c0, c1):
            acc = b_ref[:, c0:c1] + w_ref[0:1, c0:c1] * buf[pl.ds(halo - 3, q), c0:c1]
            for t in range(1, SSD_CONV):
                acc = acc + w_ref[t:t + 1, c0:c1] * buf[pl.ds(halo - 3 + t, q), c0:c1]
            return _silu(acc)

        xg = conv(xbuf, cwx_ref, cbx_ref, lo, hi)
        bg = conv(bcbuf, cwbc_ref, cbbc_ref, blo, blo + SSD_STATE)
        cg = conv(bcbuf, cwbc_ref, cbbc_ref, clo, clo + SSD_STATE)
        bg16 = bg.astype(BF16)
        cg16 = cg.astype(BF16)

        a_g = a_feat[:, lo:hi]
        a_last = a_g[q - 1:q, :]
        xdt = xg * dt_feat[:, lo:hi]

        cb = _dot_nt(cg16, bg16)
        m_parts = []
        x_parts = []
        for jj in range(SSD_HEADS_PER_GROUP):
            j = g * SSD_HEADS_PER_GROUP + jj
            seg = a_tile[:, j * q:(j + 1) * q] - a_cs_t[j:j + 1, :]
            decay = jnp.exp(jnp.where(causal, seg, -jnp.inf))
            m_parts.append((cb * decay).astype(BF16))
            in_head = (lane >= jj * SSD_HEAD_DIM) & (lane < (jj + 1) * SSD_HEAD_DIM)
            x_parts.append(jnp.where(in_head, xdt, 0.0).astype(BF16))
        y = _dot(jnp.concatenate(m_parts, axis=1), jnp.concatenate(x_parts, axis=0))

        prev = state[:, lo:hi]
        y = y + _dot(cg16, prev.astype(BF16)) * jnp.exp(a_g)
        xdd = (xdt * jnp.exp(a_last - a_g)).astype(BF16)
        state[:, lo:hi] = prev * jnp.exp(a_last) + _dot_tn(bg16, xdd)

        y = y + xg * dskip_ref[:, lo:hi]
        y = y * _silu(z_ref[:, lo:hi].astype(F32))
        ms = jnp.sum(y * y, axis=-1, keepdims=True) * (1.0 / SSD_GROUP_WIDTH)
        o_ref[:, lo:hi] = (y * lax.rsqrt(ms + RMS_EPS) * nw_ref[:, lo:hi]).astype(o_ref.dtype)

    xbuf[0:halo, :] = xbuf[q:q + halo, :]
    bcbuf[0:halo, :] = bcbuf[q:q + halo, :]


def _ssd(zx, bc, dt, consts, *, batch, seq):
    (cwx, cbx, cwbc, cbbc, dtb, alog, dskip, nw, head_to_feat, head_to_tile) = consts
    q = SSD_CHUNK
    nc = seq // q
    m = batch * seq
    wp = SSD_WIDTH_PAD

    def tok(b, c):
        return b * nc + c

    def const_spec(arr):
        return pl.BlockSpec(arr.shape, lambda b, c: (0, 0))

    return pl.pallas_call(
        _ssd_kernel,
        out_shape=jax.ShapeDtypeStruct((m, wp), BF16),
        grid=(batch, nc),
        in_specs=[pl.BlockSpec((q, wp), lambda b, c: (tok(b, c), 0)),
                  pl.BlockSpec((q, wp), lambda b, c: (tok(b, c), 1)),
                  pl.BlockSpec((q, SSD_BC_WIDTH), lambda b, c: (tok(b, c), 0)),
                  pl.BlockSpec((q, LANES), lambda b, c: (tok(b, c), 0)),
                  const_spec(cwx), const_spec(cbx), const_spec(cwbc), const_spec(cbbc),
                  const_spec(dtb), const_spec(alog), const_spec(dskip), const_spec(nw),
                  const_spec(head_to_feat), const_spec(head_to_tile)],
        out_specs=pl.BlockSpec((q, wp), lambda b, c: (tok(b, c), 0)),
        scratch_shapes=[pltpu.VMEM((q + CONV_HALO_ROWS, wp), F32),
                        pltpu.VMEM((q + CONV_HALO_ROWS, SSD_BC_WIDTH), F32),
                        pltpu.VMEM((SSD_STATE, wp), F32)],
        compiler_params=_cparams(("parallel", "arbitrary")),
        name="ssd_chunk_scan",
    )(zx, zx, bc, dt, cwx, cbx, cwbc, cbbc, dtb, alog, dskip, nw, head_to_feat, head_to_tile)


def _alibi_slopes(n):
    def pow2(m):
        start = 2.0 ** (-8.0 / m)
        return [start ** (i + 1) for i in range(m)]
    if math.log2(n).is_integer():
        s = pow2(n)
    else:
        c = 2 ** int(math.floor(math.log2(n)))
        s = pow2(c) + pow2(2 * c)[0::2][: n - c]
    return [float(np.float32(v)) for v in s]


def _attn_branch_kernel(q_ref, k_ref, kp_ref, v_ref, vp_ref, o_ref, lse_ref, *, dilation, slopes):
    n = pl.program_id(2)
    c = ATT_BLOCK
    qi = lax.broadcasted_iota(jnp.int32, (c, c), 0)
    kj = lax.broadcasted_iota(jnp.int32, (c, c), 1)
    dist_cur = (qi - kj).astype(F32)
    dist_prev = dist_cur + float(c)
    valid_cur = kj <= qi
    valid_prev = (kj >= qi) & (n > 0)
    lane = lax.broadcasted_iota(jnp.int32, (c, LANES), 1)
    lse_tile = jnp.zeros((c, LANES), F32)
    scale = ATT_HEAD_DIM ** -0.5

    for h in range(ATT_HEADS):
        sl = slice(h * ATT_HEAD_DIM, (h + 1) * ATT_HEAD_DIM)
        qh = q_ref[0, :, sl]
        tok_slope = slopes[h] * dilation
        s_cur = _dot_nt(qh, k_ref[0, :, sl]) * scale - tok_slope * dist_cur
        s_prev = _dot_nt(qh, kp_ref[0, :, sl]) * scale - tok_slope * dist_prev
        s_cur = jnp.where(valid_cur, s_cur, -jnp.inf)
        s_prev = jnp.where(valid_prev, s_prev, -jnp.inf)
        mx = jnp.maximum(jnp.max(s_cur, axis=-1, keepdims=True), jnp.max(s_prev, axis=-1, keepdims=True))
        p_cur = jnp.exp(s_cur - mx)
        p_prev = jnp.exp(s_prev - mx)
        den = jnp.sum(p_cur, axis=-1, keepdims=True) + jnp.sum(p_prev, axis=-1, keepdims=True)
        o = _dot(p_cur.astype(BF16), v_ref[0, :, sl]) + _dot(p_prev.astype(BF16), vp_ref[0, :, sl])
        o_ref[0, :, sl] = (o / den).astype(o_ref.dtype)
        lse_tile = jnp.where(lane == h, mx + jnp.log(den), lse_tile)

    lse_ref[0] = lse_tile


def _attn_branch(qkv, dilation, *, batch, seq):
    d = dilation
    sub_len = seq // d
    nb = sub_len // ATT_BLOCK
    w = ATT_WIDTH
    qkv_v = qkv.reshape(batch, sub_len, d * 3 * w)
    blk = (1, ATT_BLOCK, w)
    o, lse = pl.pallas_call(
        functools.partial(_attn_branch_kernel, dilation=d, slopes=_alibi_slopes(ATT_HEADS)),
        out_shape=(jax.ShapeDtypeStruct((batch, sub_len, d * w), BF16),
                   jax.ShapeDtypeStruct((batch, sub_len, d * LANES), F32)),
        grid=(batch, d, nb),
        in_specs=[pl.BlockSpec(blk, lambda b, r, n: (b, n, 3 * r)),
                  pl.BlockSpec(blk, lambda b, r, n: (b, n, 3 * r + 1)),
                  pl.BlockSpec(blk, lambda b, r, n: (b, jnp.maximum(n - 1, 0), 3 * r + 1)),
                  pl.BlockSpec(blk, lambda b, r, n: (b, n, 3 * r + 2)),
                  pl.BlockSpec(blk, lambda b, r, n: (b, jnp.maximum(n - 1, 0), 3 * r + 2))],
        out_specs=(pl.BlockSpec(blk, lambda b, r, n: (b, n, r)),
                   pl.BlockSpec((1, ATT_BLOCK, LANES), lambda b, r, n: (b, n, r))),
        compiler_params=_cparams(("parallel", "parallel", "arbitrary")),
        name=f"dilated_attn_d{d}",
    )(qkv_v, qkv_v, qkv_v, qkv_v, qkv_v)
    return o.reshape(batch * seq, w), lse.reshape(batch * seq, LANES)


def _attn_merge_kernel(o1_ref, o2_ref, o3_ref, l1_ref, l2_ref, l3_ref, head_to_feat_ref, o_ref):
    lses = [l1_ref[...], l2_ref[...], l3_ref[...]]
    mx = jnp.maximum(jnp.maximum(lses[0], lses[1]), lses[2])
    es = [jnp.exp(l - mx) for l in lses]
    inv = 1.0 / (es[0] + es[1] + es[2])
    sel = head_to_feat_ref[...]
    acc = None
    for e, oref in zip(es, (o1_ref, o2_ref, o3_ref)):
        term = _dot_split_lhs(e * inv, sel) * oref[...].astype(F32)
        acc = term if acc is None else acc + term
    o_ref[...] = acc.astype(o_ref.dtype)


def _attn_merge(outs, lses, head_to_feat, *, tm=256):
    m, w = outs[0].shape
    ospec = pl.BlockSpec((tm, w), lambda i: (i, 0))
    lspec = pl.BlockSpec((tm, LANES), lambda i: (i, 0))
    return pl.pallas_call(
        _attn_merge_kernel,
        out_shape=jax.ShapeDtypeStruct((m, w), BF16),
        grid=(m // tm,),
        in_specs=[ospec, ospec, ospec, lspec, lspec, lspec,
                  pl.BlockSpec(head_to_feat.shape, lambda i: (0, 0))],
        out_specs=ospec,
        compiler_params=_cparams(("parallel",)),
        name="attn_merge",
    )(*outs, *lses, head_to_feat)


def _pad_groups(t):
    lead = t.shape[:-1]
    t = t.reshape(*lead, SSD_GROUPS, SSD_GROUP_WIDTH)
    t = jnp.pad(t, [(0, 0)] * len(lead) + [(0, 0), (0, SSD_GROUP_PAD - SSD_GROUP_WIDTH)])
    return t.reshape(*lead, SSD_WIDTH_PAD)


def _pad_last(t, n):
    return jnp.pad(t, [(0, 0)] * (t.ndim - 1) + [(0, n - t.shape[-1])])


def _head_selectors():
    head_to_feat = np.zeros((LANES, SSD_WIDTH_PAD), np.float32)
    head_to_tile = np.zeros((LANES, SSD_HEADS * SSD_CHUNK), np.float32)
    for j in range(SSD_HEADS):
        g, jj = divmod(j, SSD_HEADS_PER_GROUP)
        f0 = g * SSD_GROUP_PAD + jj * SSD_HEAD_DIM
        head_to_feat[j, f0:f0 + SSD_HEAD_DIM] = 1.0
        head_to_tile[j, j * SSD_CHUNK:(j + 1) * SSD_CHUNK] = 1.0
    att_head_to_feat = np.zeros((LANES, ATT_WIDTH), np.float32)
    for h in range(ATT_HEADS):
        att_head_to_feat[h, h * ATT_HEAD_DIM:(h + 1) * ATT_HEAD_DIM] = 1.0
    return (jnp.asarray(head_to_feat, BF16), jnp.asarray(head_to_tile, BF16), jnp.asarray(att_head_to_feat, BF16))


def _ffn(h, xn, post_w, w_gate, w_up, w_down, next_pre_w):
    act = _gateup(xn, w_gate, w_up)
    return _matmul_postnorm(act, w_down.astype(BF16), h, post_w, 0.5, next_pre_w, name="ffn_down_postnorm")


def kernel(x, p, ffn1_pre_w, ffn1_post_w, ffn1_w_gate, ffn1_w_up, ffn1_w_down, mix_pre_w, mix_post_w, w_in, conv_w, conv_b, dt_bias, a_log, d_skip, ssd_norm_w, w_out, ffn2_pre_w, ffn2_post_w, ffn2_w_gate, ffn2_w_up, ffn2_w_down, ple_pre_w, ple_post_w, w_ple_gate, w_ple_proj):
    batch, seq, d_model = x.shape
    depth = w_in.shape[0]
    m = batch * seq
    h = x.reshape(m, d_model)
    head_to_feat, head_to_tile, att_head_to_feat = _head_selectors()
    i1 = SSD_WIDTH
    i2 = i1 + SSD_WIDTH
    i3 = i2 + SSD_BC_WIDTH
    i4 = i3 + SSD_HEADS

    for i in range(depth):
        xn = _rmsnorm(h, ffn1_pre_w[i])
        h, u = _ffn(h, xn, ffn1_post_w[i], ffn1_w_gate[i], ffn1_w_up[i], ffn1_w_down[i], mix_pre_w[i])

        wi = w_in[i]
        w_zx = jnp.concatenate([_pad_groups(wi[:, :i1]), _pad_groups(wi[:, i1:i2])], axis=1).astype(BF16)
        w_bc = wi[:, i2:i3].astype(BF16)
        w_dt = _pad_last(wi[:, i3:i4], LANES).astype(BF16)
        w_qkv = wi[:, i4:].astype(BF16)
        zx = _matmul(u, w_zx, BF16, name="in_proj_zx")
        bc = _matmul(u, w_bc, BF16, name="in_proj_bc")
        dt = _matmul(u, w_dt, F32, name="in_proj_dt")
        qkv = _matmul(u, w_qkv, BF16, name="in_proj_qkv")

        cw, cb = conv_w[i], conv_b[i]
        ssd_consts = (
            _pad_groups(cw[:, :SSD_WIDTH]), _pad_groups(cb[None, :SSD_WIDTH]),
            cw[:, SSD_WIDTH:], cb[None, SSD_WIDTH:],
            _pad_last(dt_bias[i][None, :], LANES),
            _pad_last(a_log[i][None, :], LANES),
            _pad_groups(jnp.repeat(d_skip[i], SSD_HEAD_DIM)[None, :]),
            _pad_groups(ssd_norm_w[i][None, :]),
            head_to_feat, head_to_tile)
        y_ssd = _ssd(zx, bc, dt, ssd_consts, batch=batch, seq=seq)

        outs, lses = [], []
        for _, dilation in ATT_BRANCHES:
            o, lse = _attn_branch(qkv, dilation, batch=batch, seq=seq)
            outs.append(o)
            lses.append(lse)
        att = _attn_merge(outs, lses, att_head_to_feat)

        wo = w_out[i]
        wo_ssd = jnp.pad(wo[:SSD_WIDTH].reshape(SSD_GROUPS, SSD_GROUP_WIDTH, d_model),
                         ((0, 0), (0, SSD_GROUP_PAD - SSD_GROUP_WIDTH), (0, 0))).reshape(SSD_WIDTH_PAD, d_model)
        wo_pad = jnp.concatenate([wo_ssd, wo[SSD_WIDTH:]], axis=0).astype(BF16)
        mixed = jnp.concatenate([y_ssd, att], axis=1)
        h, xn = _matmul_postnorm(mixed, wo_pad, h, mix_post_w[i], 1.0, ffn2_pre_w[i], tn=512,
                                 name="out_proj_postnorm")

        h, xn = _ffn(h, xn, ffn2_post_w[i], ffn2_w_gate[i], ffn2_w_up[i], ffn2_w_down[i], ple_pre_w[i])

        h = _ple(xn, w_ple_gate[i].astype(BF16), h, ple_post_w[i], p[i].reshape(m, -1), w_ple_proj[i].astype(BF16))

    return h.reshape(batch, seq, d_model)
```

```python
import functools
import math

import numpy as np
import jax
import jax.numpy as jnp
from jax import lax
from jax.experimental import pallas as pl
from jax.experimental.pallas import tpu as pltpu

F32 = jnp.float32
BF16 = jnp.bfloat16

RMS_EPS = 1e-6

LANES = 128
SUBLANES = 8
VMEM_LIMIT_BYTES = 58 * 1024 * 1024

SSD_HEAD_DIM = 64
SSD_HEADS = 40
SSD_GROUPS = 8
SSD_HEADS_PER_GROUP = SSD_HEADS // SSD_GROUPS
SSD_GROUP_WIDTH = SSD_HEADS_PER_GROUP * SSD_HEAD_DIM
SSD_GROUP_PAD = 384
SSD_WIDTH = SSD_HEADS * SSD_HEAD_DIM
SSD_WIDTH_PAD = SSD_GROUPS * SSD_GROUP_PAD
SSD_STATE = 128
SSD_CONV = 4
SSD_CHUNK = 128
SSD_BC_WIDTH = 2 * SSD_GROUPS * SSD_STATE
CONV_HALO_ROWS = SUBLANES

ATT_HEAD_DIM = 128
ATT_HEADS = 12
ATT_WIDTH = ATT_HEADS * ATT_HEAD_DIM
ATT_BRANCHES = ((128, 1), (512, 4), (2048, 16))
ATT_BLOCK = 128


def _cparams(semantics):
    return pltpu.CompilerParams(dimension_semantics=semantics, vmem_limit_bytes=VMEM_LIMIT_BYTES)


def _dot(a, b):
    return jnp.dot(a, b, preferred_element_type=F32)


def _dot_nt(a, b):
    return lax.dot_general(a, b, (((1,), (1,)), ((), ())), preferred_element_type=F32)


def _dot_tn(a, b):
    return lax.dot_general(a, b, (((0,), (0,)), ((), ())), preferred_element_type=F32)


def _split_bf16(x):
    hi = x.astype(BF16)
    lo = (x - hi.astype(F32)).astype(BF16)
    return hi, lo


def _dot_split_lhs(x, sel):
    hi, lo = _split_bf16(x)
    return _dot(hi, sel) + _dot(lo, sel)


def _silu(x):
    hx = 0.5 * x
    return hx + hx * jnp.tanh(hx)


def _rms_scale(v):
    return lax.rsqrt(jnp.mean(v * v, axis=-1, keepdims=True) + RMS_EPS)


def _row_slabs(n_rows, rows_per_step, body, unroll=1):
    def step(r, carry):
        body(pl.ds(pl.multiple_of(r * rows_per_step, rows_per_step), rows_per_step))
        return carry

    lax.fori_loop(0, n_rows // rows_per_step, step, 0, unroll=unroll)


def _rmsnorm_kernel(x_ref, w_ref, o_ref):
    def body(rows):
        x = x_ref[rows, :]
        o_ref[rows, :] = (x * _rms_scale(x) * w_ref[...]).astype(o_ref.dtype)

    _row_slabs(x_ref.shape[0], 16, body, unroll=4)


def _rmsnorm(x, w, *, tm=256):
    m, d = x.shape
    return pl.pallas_call(
        _rmsnorm_kernel,
        out_shape=jax.ShapeDtypeStruct((m, d), BF16),
        grid=(m // tm,),
        in_specs=[pl.BlockSpec((tm, d), lambda i: (i, 0)),
                  pl.BlockSpec((1, d), lambda i: (0, 0))],
        out_specs=pl.BlockSpec((tm, d), lambda i: (i, 0)),
        compiler_params=_cparams(("parallel",)),
        name="rmsnorm",
    )(x, w.reshape(1, d))


def _matmul_kernel(a_ref, w_ref, o_ref, *, k_steps):
    part = _dot(a_ref[...], w_ref[...])
    if k_steps == 1:
        o_ref[...] = part.astype(o_ref.dtype)
    else:
        k = pl.program_id(2)

        @pl.when(k == 0)
        def _():
            o_ref[...] = part

        @pl.when(k > 0)
        def _():
            o_ref[...] += part


def _matmul(a, w, out_dtype, *, tm=1024, tn=512, k_steps=1, name="matmul"):
    m, k = a.shape
    n = w.shape[1]
    tn = min(tn, n)
    tk = k // k_steps
    assert k_steps == 1 or out_dtype == F32
    return pl.pallas_call(
        functools.partial(_matmul_kernel, k_steps=k_steps),
        out_shape=jax.ShapeDtypeStruct((m, n), out_dtype),
        grid=(m // tm, n // tn, k_steps),
        in_specs=[pl.BlockSpec((tm, tk), lambda i, j, kk: (i, kk)),
                  pl.BlockSpec((tk, tn), lambda i, j, kk: (kk, j))],
        out_specs=pl.BlockSpec((tm, tn), lambda i, j, kk: (i, j)),
        compiler_params=_cparams(("parallel", "arbitrary", "arbitrary")),
        name=name,
    )(a, w)


def _gateup_kernel(a_ref, wg_ref, wu_ref, o_ref):
    a = a_ref[...]
    g = _dot(a, wg_ref[...].astype(BF16))
    u = _dot(a, wu_ref[...].astype(BF16))
    o_ref[...] = (_silu(g) * u).astype(o_ref.dtype)


def _gateup(a, wg, wu, *, tm=2048, tn=256):
    m, k = a.shape
    n = wg.shape[1]
    return pl.pallas_call(
        _gateup_kernel,
        out_shape=jax.ShapeDtypeStruct((m, n), BF16),
        grid=(m // tm, n // tn),
        in_specs=[pl.BlockSpec((tm, k), lambda i, j: (i, 0), pipeline_mode=pl.Buffered(1)),
                  pl.BlockSpec((k, tn), lambda i, j: (0, j)),
                  pl.BlockSpec((k, tn), lambda i, j: (0, j))],
        out_specs=pl.BlockSpec((tm, tn), lambda i, j: (i, j)),
        compiler_params=_cparams(("parallel", "arbitrary")),
        name="swiglu_gateup",
    )(a, wg, wu)


def _postnorm_kernel(f_ref, h_ref, nw_ref, next_w_ref, o_ref, xn_ref, *, scale):
    def body(rows):
        f = f_ref[rows, :]
        hn = h_ref[rows, :] + scale * (f * _rms_scale(f) * nw_ref[...])
        o_ref[rows, :] = hn
        xn_ref[rows, :] = (hn * _rms_scale(hn) * next_w_ref[...]).astype(xn_ref.dtype)

    _row_slabs(f_ref.shape[0], 16, body, unroll=4)


def _postnorm(f, h, nw, scale, next_w, *, tm=256):
    m, n = f.shape
    row_spec = pl.BlockSpec((tm, n), lambda i: (i, 0))
    vec_spec = pl.BlockSpec((1, n), lambda i: (0, 0))
    return pl.pallas_call(
        functools.partial(_postnorm_kernel, scale=scale),
        out_shape=(jax.ShapeDtypeStruct((m, n), F32), jax.ShapeDtypeStruct((m, n), BF16)),
        grid=(m // tm,),
        in_specs=[row_spec, row_spec, vec_spec, vec_spec],
        out_specs=(row_spec, row_spec),
        compiler_params=_cparams(("parallel",)),
        name="postnorm_residual",
    )(f, h, nw.reshape(1, n), next_w.reshape(1, n))


def _ple_kernel(g_ref, p_ref, wp_ref, h_ref, nw_ref, o_ref):
    def body(rows):
        proj = _dot(p_ref[rows, :].astype(BF16), wp_ref[...])
        f = jax.nn.sigmoid(g_ref[rows, :]) * proj
        o_ref[rows, :] = h_ref[rows, :] + f * _rms_scale(f) * nw_ref[...]

    _row_slabs(g_ref.shape[0], 64, body)


def _ple(g, p, wp, h, nw, *, tm=256):
    m, n = g.shape
    pdim = p.shape[1]
    row_spec = pl.BlockSpec((tm, n), lambda i: (i, 0))
    return pl.pallas_call(
        _ple_kernel,
        out_shape=jax.ShapeDtypeStruct((m, n), F32),
        grid=(m // tm,),
        in_specs=[row_spec,
                  pl.BlockSpec((tm, pdim), lambda i: (i, 0)),
                  pl.BlockSpec((pdim, n), lambda i: (0, 0)),
                  row_spec,
                  pl.BlockSpec((1, n), lambda i: (0, 0))],
        out_specs=row_spec,
        compiler_params=_cparams(("parallel",)),
        name="ple_gate_postnorm",
    )(g, p, wp, h, nw.reshape(1, n))


def _ssd_kernel(z_ref, x_ref, bc_ref, dt_ref, cwx_ref, cbx_ref, cwbc_ref, cbbc_ref, dtb_ref, alog_ref,
                dskip_ref, nw_ref, head_to_feat_ref, head_to_tile_ref, o_ref, xbuf, bcbuf, state):
    c = pl.program_id(1)
    q = SSD_CHUNK
    halo = CONV_HALO_ROWS

    @pl.when(c == 0)
    def _():
        xbuf[0:halo, :] = jnp.zeros((halo, xbuf.shape[1]), F32)
        bcbuf[0:halo, :] = jnp.zeros((halo, bcbuf.shape[1]), F32)
        state[...] = jnp.zeros(state.shape, F32)

    xbuf[halo:halo + q, :] = x_ref[...].astype(F32)
    bcbuf[halo:halo + q, :] = bc_ref[...].astype(F32)

    v = dt_ref[...] + dtb_ref[...]
    dt = jnp.maximum(v, 0.0) + jnp.log(1.0 + jnp.exp(-jnp.abs(v)))
    da = dt * -jnp.exp(alog_ref[...])
    row = lax.broadcasted_iota(jnp.int32, (q, q), 0)
    col = lax.broadcasted_iota(jnp.int32, (q, q), 1)
    causal = row >= col
    tril = jnp.where(causal, 1.0, 0.0).astype(BF16)
    da_hi, da_lo = _split_bf16(da)
    a_cs = _dot(tril, da_hi) + _dot(tril, da_lo)
    a_cs_t = a_cs.T
    head_to_feat = head_to_feat_ref[...]
    a_feat = _dot_split_lhs(a_cs, head_to_feat)
    dt_feat = _dot_split_lhs(dt, head_to_feat)
    a_tile = _dot_split_lhs(a_cs, head_to_tile_ref[...])

    lane = lax.broadcasted_iota(jnp.int32, (q, LANES), 1)
    heads_per_tile = LANES // SSD_HEAD_DIM

    for g in range(SSD_GROUPS):
        lo, hi = g * SSD_GROUP_PAD, (g + 1) * SSD_GROUP_PAD
        blo, clo = g * SSD_STATE, (SSD_GROUPS + g) * SSD_STATE

        def conv(buf, w_ref, b_ref, c0, c1):
            acc = b_ref[:, c0:c1] + w_ref[0:1, c0:c1] * buf[pl.ds(halo - 3, q), c0:c1]
            for t in range(1, SSD_CONV):
                acc = acc + w_ref[t:t + 1, c0:c1] * buf[pl.ds(halo - 3 + t, q), c0:c1]
            return _silu(acc)

        xg = conv(xbuf, cwx_ref, cbx_ref, lo, hi)
        bg = conv(bcbuf, cwbc_ref, cbbc_ref, blo, blo + SSD_STATE)
        cg = conv(bcbuf, cwbc_ref, cbbc_ref, clo, clo + SSD_STATE)
        bg16 = bg.astype(BF16)
        cg16 = cg.astype(BF16)

        a_g = a_feat[:, lo:hi]
        a_last = a_g[q - 1:q, :]
        xdt = xg * dt_feat[:, lo:hi]

        cb = _dot_nt(cg16, bg16)
        m_parts = []
        for jj in range(SSD_HEADS_PER_GROUP):
            j = g * SSD_HEADS_PER_GROUP + jj
            seg = a_tile[:, j * q:(j + 1) * q] - a_cs_t[j:j + 1, :]
            decay = jnp.exp(jnp.where(causal, seg, -jnp.inf))
            m_parts.append((cb * decay).astype(BF16))
        y_tiles = []
        for t in range(SSD_GROUP_PAD // LANES):
            xt = xdt[:, t * LANES:(t + 1) * LANES]
            heads = [jj for jj in range(t * heads_per_tile, (t + 1) * heads_per_tile) if jj < SSD_HEADS_PER_GROUP]
            if len(heads) == 1:
                y_tiles.append(_dot(m_parts[heads[0]], xt.astype(BF16)))
            else:
                xs = [jnp.where(lane // SSD_HEAD_DIM == jj % heads_per_tile, xt, 0.0).astype(BF16) for jj in heads]
                y_tiles.append(_dot(jnp.concatenate([m_parts[jj] for jj in heads], axis=1),
                                    jnp.concatenate(xs, axis=0)))
        y = jnp.concatenate(y_tiles, axis=1)

        prev = state[:, lo:hi]
        y = y + _dot(cg16, prev.astype(BF16)) * jnp.exp(a_g)
        xdd = (xdt * jnp.exp(a_last - a_g)).astype(BF16)
        state[:, lo:hi] = prev * jnp.exp(a_last) + _dot_tn(bg16, xdd)

        y = y + xg * dskip_ref[:, lo:hi]
        y = y * _silu(z_ref[:, lo:hi].astype(F32))
        ms = jnp.sum(y * y, axis=-1, keepdims=True) * (1.0 / SSD_GROUP_WIDTH)
        o_ref[:, lo:hi] = (y * lax.rsqrt(ms + RMS_EPS) * nw_ref[:, lo:hi]).astype(o_ref.dtype)

    xbuf[0:halo, :] = xbuf[q:q + halo, :]
    bcbuf[0:halo, :] = bcbuf[q:q + halo, :]


def _ssd(zx, bc, dt, consts, *, batch, seq, out_width=SSD_WIDTH_PAD):
    (cwx, cbx, cwbc, cbbc, dtb, alog, dskip, nw, head_to_feat, head_to_tile) = consts
    q = SSD_CHUNK
    nc = seq // q
    m = batch * seq
    wp = SSD_WIDTH_PAD

    def tok(b, c):
        return b * nc + c

    def const_spec(arr):
        return pl.BlockSpec(arr.shape, lambda b, c: (0, 0))

    return pl.pallas_call(
        _ssd_kernel,
        out_shape=jax.ShapeDtypeStruct((m, out_width), BF16),
        grid=(batch, nc),
        in_specs=[pl.BlockSpec((q, wp), lambda b, c: (tok(b, c), 0)),
                  pl.BlockSpec((q, wp), lambda b, c: (tok(b, c), 1)),
                  pl.BlockSpec((q, SSD_BC_WIDTH), lambda b, c: (tok(b, c), 0)),
                  pl.BlockSpec((q, LANES), lambda b, c: (tok(b, c), 0)),
                  const_spec(cwx), const_spec(cbx), const_spec(cwbc), const_spec(cbbc),
                  const_spec(dtb), const_spec(alog), const_spec(dskip), const_spec(nw),
                  const_spec(head_to_feat), const_spec(head_to_tile)],
        out_specs=pl.BlockSpec((q, wp), lambda b, c: (tok(b, c), 0)),
        scratch_shapes=[pltpu.VMEM((q + CONV_HALO_ROWS, wp), F32),
                        pltpu.VMEM((q + CONV_HALO_ROWS, SSD_BC_WIDTH), F32),
                        pltpu.VMEM((SSD_STATE, wp), F32)],
        compiler_params=_cparams(("parallel", "arbitrary")),
        name="ssd_chunk_scan",
    )(zx, zx, bc, dt, cwx, cbx, cwbc, cbbc, dtb, alog, dskip, nw, head_to_feat, head_to_tile)


def _alibi_slopes(n):
    def pow2(m):
        start = 2.0 ** (-8.0 / m)
        return [start ** (i + 1) for i in range(m)]
    if math.log2(n).is_integer():
        s = pow2(n)
    else:
        c = 2 ** int(math.floor(math.log2(n)))
        s = pow2(c) + pow2(2 * c)[0::2][: n - c]
    return [float(np.float32(v)) for v in s]


def _attn_branch_kernel(q_ref, k_ref, kp_ref, v_ref, vp_ref, o_ref, lse_ref, *, dilation, slopes):
    n = pl.program_id(2)
    qb = q_ref.shape[1]
    c = ATT_BLOCK
    neg_tok = -float(dilation)

    def alibi_base(shape, key_offset, extra_valid=None):
        qi = lax.broadcasted_iota(jnp.int32, shape, 0)
        kj = lax.broadcasted_iota(jnp.int32, shape, 1)
        dist = qi - kj + key_offset
        valid = (dist >= 0) & (dist <= c)
        if extra_valid is not None:
            valid = valid & extra_valid
        return jnp.where(valid, neg_tok * dist.astype(F32), -jnp.inf)

    base_cur = alibi_base((qb, qb), 0)
    base_prev = alibi_base((qb, c), c, n > 0)
    lane = lax.broadcasted_iota(jnp.int32, (qb, LANES), 1)
    lse_tile = jnp.zeros((qb, LANES), F32)

    for h in range(ATT_HEADS):
        sl = slice(h * ATT_HEAD_DIM, (h + 1) * ATT_HEAD_DIM)
        qh = q_ref[0, :, sl]
        s_cur = _dot_nt(qh, k_ref[0, :, sl]) + slopes[h] * base_cur
        s_prev = _dot_nt(qh, kp_ref[0, :, sl]) + slopes[h] * base_prev
        mx = jnp.maximum(jnp.max(s_cur, axis=-1, keepdims=True), jnp.max(s_prev, axis=-1, keepdims=True))
        p_cur = jnp.exp(s_cur - mx)
        p_prev = jnp.exp(s_prev - mx)
        den = jnp.sum(p_cur, axis=-1, keepdims=True) + jnp.sum(p_prev, axis=-1, keepdims=True)
        o = _dot(p_cur.astype(BF16), v_ref[0, :, sl]) + _dot(p_prev.astype(BF16), vp_ref[0, :, sl])
        o_ref[0, :, sl] = (o / den).astype(o_ref.dtype)
        lse_tile = jnp.where(lane == h, mx + jnp.log(den), lse_tile)

    lse_ref[0] = lse_tile


def _attn_branch(qkv, dilation, *, batch, seq, max_query_block=512):
    d = dilation
    sub_len = seq // d
    qb = min(max_query_block, sub_len)
    nb = sub_len // qb
    per = qb // ATT_BLOCK
    w = ATT_WIDTH
    qkv_v = qkv.reshape(batch, sub_len, d * 3 * w)
    blk = (1, qb, w)
    prev_blk = (1, ATT_BLOCK, w)

    def prev_map(col):
        return lambda b, r, n: (b, jnp.maximum(n * per - 1, 0), 3 * r + col)

    o, lse = pl.pallas_call(
        functools.partial(_attn_branch_kernel, dilation=d, slopes=_alibi_slopes(ATT_HEADS)),
        out_shape=(jax.ShapeDtypeStruct((batch, sub_len, d * w), BF16),
                   jax.ShapeDtypeStruct((batch, sub_len, d * LANES), F32)),
        grid=(batch, d, nb),
        in_specs=[pl.BlockSpec(blk, lambda b, r, n: (b, n, 3 * r)),
                  pl.BlockSpec(blk, lambda b, r, n: (b, n, 3 * r + 1)),
                  pl.BlockSpec(prev_blk, prev_map(1)),
                  pl.BlockSpec(blk, lambda b, r, n: (b, n, 3 * r + 2)),
                  pl.BlockSpec(prev_blk, prev_map(2))],
        out_specs=(pl.BlockSpec(blk, lambda b, r, n: (b, n, r)),
                   pl.BlockSpec((1, qb, LANES), lambda b, r, n: (b, n, r))),
        compiler_params=_cparams(("parallel", "parallel", "arbitrary")),
        name=f"dilated_attn_d{d}",
    )(qkv_v, qkv_v, qkv_v, qkv_v, qkv_v)
    return o.reshape(batch * seq, w), lse.reshape(batch * seq, LANES)


def _attn_merge_kernel(o1_ref, o2_ref, o3_ref, l1_ref, l2_ref, l3_ref, head_to_feat_ref, mixed_ref, o_ref):
    del mixed_ref
    lses = [l1_ref[...], l2_ref[...], l3_ref[...]]
    mx = jnp.maximum(jnp.maximum(lses[0], lses[1]), lses[2])
    es = [jnp.exp(l - mx) for l in lses]
    inv = 1.0 / (es[0] + es[1] + es[2])
    sel = head_to_feat_ref[...]
    acc = None
    for e, oref in zip(es, (o1_ref, o2_ref, o3_ref)):
        term = _dot_split_lhs(e * inv, sel) * oref[...].astype(F32)
        acc = term if acc is None else acc + term
    o_ref[...] = acc.astype(o_ref.dtype)


def _attn_merge(outs, lses, head_to_feat, mixed, *, tm=256):
    m, w = outs[0].shape
    col_block = mixed.shape[1] // w - 1
    assert (col_block + 1) * w == mixed.shape[1]
    ospec = pl.BlockSpec((tm, w), lambda i: (i, 0))
    lspec = pl.BlockSpec((tm, LANES), lambda i: (i, 0))
    return pl.pallas_call(
        _attn_merge_kernel,
        out_shape=jax.ShapeDtypeStruct(mixed.shape, mixed.dtype),
        grid=(m // tm,),
        in_specs=[ospec, ospec, ospec, lspec, lspec, lspec,
                  pl.BlockSpec(head_to_feat.shape, lambda i: (0, 0)),
                  pl.BlockSpec(memory_space=pl.ANY)],
        out_specs=pl.BlockSpec((tm, w), lambda i: (i, col_block)),
        input_output_aliases={7: 0},
        compiler_params=_cparams(("parallel",)),
        name="attn_merge",
    )(*outs, *lses, head_to_feat, mixed)


def _pad_groups(t):
    lead = t.shape[:-1]
    t = t.reshape(*lead, SSD_GROUPS, SSD_GROUP_WIDTH)
    t = jnp.pad(t, [(0, 0)] * len(lead) + [(0, 0), (0, SSD_GROUP_PAD - SSD_GROUP_WIDTH)])
    return t.reshape(*lead, SSD_WIDTH_PAD)


def _pad_last(t, n):
    return jnp.pad(t, [(0, 0)] * (t.ndim - 1) + [(0, n - t.shape[-1])])


def _head_selectors():
    head_to_feat = np.zeros((LANES, SSD_WIDTH_PAD), np.float32)
    head_to_tile = np.zeros((LANES, SSD_HEADS * SSD_CHUNK), np.float32)
    for j in range(SSD_HEADS):
        g, jj = divmod(j, SSD_HEADS_PER_GROUP)
        f0 = g * SSD_GROUP_PAD + jj * SSD_HEAD_DIM
        head_to_feat[j, f0:f0 + SSD_HEAD_DIM] = 1.0
        head_to_tile[j, j * SSD_CHUNK:(j + 1) * SSD_CHUNK] = 1.0
    att_head_to_feat = np.zeros((LANES, ATT_WIDTH), np.float32)
    for h in range(ATT_HEADS):
        att_head_to_feat[h, h * ATT_HEAD_DIM:(h + 1) * ATT_HEAD_DIM] = 1.0
    return (jnp.asarray(head_to_feat, BF16), jnp.asarray(head_to_tile, BF16), jnp.asarray(att_head_to_feat, BF16))


def _ffn(h, xn, post_w, w_gate, w_up, w_down, next_pre_w):
    act = _gateup(xn, w_gate, w_up)
    f = _matmul(act, w_down.astype(BF16), F32, k_steps=2, name="ffn_down")
    return _postnorm(f, h, post_w, 0.5, next_pre_w)


def kernel(x, p, ffn1_pre_w, ffn1_post_w, ffn1_w_gate, ffn1_w_up, ffn1_w_down, mix_pre_w, mix_post_w, w_in, conv_w, conv_b, dt_bias, a_log, d_skip, ssd_norm_w, w_out, ffn2_pre_w, ffn2_post_w, ffn2_w_gate, ffn2_w_up, ffn2_w_down, ple_pre_w, ple_post_w, w_ple_gate, w_ple_proj):
    batch, seq, d_model = x.shape
    depth = w_in.shape[0]
    m = batch * seq
    h = x.reshape(m, d_model)
    head_to_feat, head_to_tile, att_head_to_feat = _head_selectors()
    i1 = SSD_WIDTH
    i2 = i1 + SSD_WIDTH
    i3 = i2 + SSD_BC_WIDTH
    i4 = i3 + SSD_HEADS

    for i in range(depth):
        xn = _rmsnorm(h, ffn1_pre_w[i])
        h, u = _ffn(h, xn, ffn1_post_w[i], ffn1_w_gate[i], ffn1_w_up[i], ffn1_w_down[i], mix_pre_w[i])

        wi = w_in[i]
        w_zx = jnp.concatenate([_pad_groups(wi[:, :i1]), _pad_groups(wi[:, i1:i2])], axis=1).astype(BF16)
        w_bc = wi[:, i2:i3].astype(BF16)
        w_dt = _pad_last(wi[:, i3:i4], LANES).astype(BF16)
        i5 = i4 + ATT_WIDTH
        w_qkv = jnp.concatenate([wi[:, i4:i5] * (ATT_HEAD_DIM ** -0.5), wi[:, i5:]], axis=1).astype(BF16)
        zx = _matmul(u, w_zx, BF16, name="in_proj_zx")
        bc = _matmul(u, w_bc, BF16, name="in_proj_bc")
        dt = _matmul(u, w_dt, F32, name="in_proj_dt")
        qkv = _matmul(u, w_qkv, BF16, name="in_proj_qkv")

        cw, cb = conv_w[i], conv_b[i]
        ssd_consts = (
            _pad_groups(cw[:, :SSD_WIDTH]), _pad_groups(cb[None, :SSD_WIDTH]),
            cw[:, SSD_WIDTH:], cb[None, SSD_WIDTH:],
            _pad_last(dt_bias[i][None, :], LANES),
            _pad_last(a_log[i][None, :], LANES),
            _pad_groups(jnp.repeat(d_skip[i], SSD_HEAD_DIM)[None, :]),
            _pad_groups(ssd_norm_w[i][None, :]),
            head_to_feat, head_to_tile)
        mixed = _ssd(zx, bc, dt, ssd_consts, batch=batch, seq=seq, out_width=SSD_WIDTH_PAD + ATT_WIDTH)

        outs, lses = [], []
        for _, dilation in ATT_BRANCHES:
            o, lse = _attn_branch(qkv, dilation, batch=batch, seq=seq)
            outs.append(o)
            lses.append(lse)
        mixed = _attn_merge(outs, lses, att_head_to_feat, mixed)

        wo = w_out[i]
        wo_ssd = jnp.pad(wo[:SSD_WIDTH].reshape(SSD_GROUPS, SSD_GROUP_WIDTH, d_model),
                         ((0, 0), (0, SSD_GROUP_PAD - SSD_GROUP_WIDTH), (0, 0))).reshape(SSD_WIDTH_PAD, d_model)
        wo_pad = jnp.concatenate([wo_ssd, wo[SSD_WIDTH:]], axis=0).astype(BF16)
        mix = _matmul(mixed, wo_pad, F32, name="out_proj")
        h, xn = _postnorm(mix, h, mix_post_w[i], 1.0, ffn2_pre_w[i])

        h, xn = _ffn(h, xn, ffn2_post_w[i], ffn2_w_gate[i], ffn2_w_up[i], ffn2_w_down[i], ple_pre_w[i])

        gate = _matmul(xn, w_ple_gate[i].astype(BF16), F32, name="ple_gate")
        h = _ple(gate, p[i].reshape(m, -1), w_ple_proj[i].astype(BF16), h, ple_post_w[i])

    return h.reshape(batch, seq, d_model)
```

```python
import functools
import math

import numpy as np
import jax
import jax.numpy as jnp
from jax import lax
from jax.experimental import pallas as pl
from jax.experimental.pallas import tpu as pltpu

F32 = jnp.float32
BF16 = jnp.bfloat16

RMS_EPS = 1e-6

LANES = 128
SUBLANES = 8
VMEM_LIMIT_BYTES = 58 * 1024 * 1024

SSD_HEAD_DIM = 64
SSD_HEADS = 40
SSD_GROUPS = 8
SSD_HEADS_PER_GROUP = SSD_HEADS // SSD_GROUPS
SSD_GROUP_WIDTH = SSD_HEADS_PER_GROUP * SSD_HEAD_DIM
SSD_GROUP_PAD = 384
SSD_WIDTH = SSD_HEADS * SSD_HEAD_DIM
SSD_WIDTH_PAD = SSD_GROUPS * SSD_GROUP_PAD
SSD_STATE = 128
SSD_CONV = 4
SSD_CHUNK = 128
SSD_BC_WIDTH = 2 * SSD_GROUPS * SSD_STATE
CONV_HALO_ROWS = SUBLANES

ATT_HEAD_DIM = 128
ATT_HEADS = 12
ATT_WIDTH = ATT_HEADS * ATT_HEAD_DIM
ATT_BRANCHES = ((128, 1), (512, 4), (2048, 16))
ATT_BLOCK = 128
ATT_QUERY_BLOCK = 512
ATT_MERGE_ROWS = 256


def _cparams(semantics):
    return pltpu.CompilerParams(dimension_semantics=semantics, vmem_limit_bytes=VMEM_LIMIT_BYTES)


def _dot(a, b):
    return jnp.dot(a, b, preferred_element_type=F32)


def _dot_nt(a, b):
    return lax.dot_general(a, b, (((1,), (1,)), ((), ())), preferred_element_type=F32)


def _dot_tn(a, b):
    return lax.dot_general(a, b, (((0,), (0,)), ((), ())), preferred_element_type=F32)


def _split_bf16(x):
    hi = x.astype(BF16)
    lo = (x - hi.astype(F32)).astype(BF16)
    return hi, lo


def _dot_split_lhs(x, sel):
    hi, lo = _split_bf16(x)
    return _dot(hi, sel) + _dot(lo, sel)


def _silu(x):
    hx = 0.5 * x
    return hx + hx * jnp.tanh(hx)


def _rms_scale(v):
    return lax.rsqrt(jnp.mean(v * v, axis=-1, keepdims=True) + RMS_EPS)


def _row_slabs(n_rows, rows_per_step, body, unroll=1):
    def step(r, carry):
        body(pl.ds(pl.multiple_of(r * rows_per_step, rows_per_step), rows_per_step))
        return carry

    lax.fori_loop(0, n_rows // rows_per_step, step, 0, unroll=unroll)


def _rmsnorm_kernel(x_ref, w_ref, o_ref):
    def body(rows):
        x = x_ref[rows, :]
        o_ref[rows, :] = (x * _rms_scale(x) * w_ref[...]).astype(o_ref.dtype)

    _row_slabs(x_ref.shape[0], 16, body, unroll=4)


def _rmsnorm(x, w, *, tm=256):
    m, d = x.shape
    return pl.pallas_call(
        _rmsnorm_kernel,
        out_shape=jax.ShapeDtypeStruct((m, d), BF16),
        grid=(m // tm,),
        in_specs=[pl.BlockSpec((tm, d), lambda i: (i, 0)),
                  pl.BlockSpec((1, d), lambda i: (0, 0))],
        out_specs=pl.BlockSpec((tm, d), lambda i: (i, 0)),
        compiler_params=_cparams(("parallel",)),
        name="rmsnorm",
    )(x, w.reshape(1, d))


def _matmul_kernel(a_ref, w_ref, o_ref, *, k_steps):
    part = _dot(a_ref[...], w_ref[...])
    if k_steps == 1:
        o_ref[...] = part.astype(o_ref.dtype)
    else:
        k = pl.program_id(2)

        @pl.when(k == 0)
        def _():
            o_ref[...] = part

        @pl.when(k > 0)
        def _():
            o_ref[...] += part


def _matmul(a, w, out_dtype, *, tm=1024, tn=512, k_steps=1, name="matmul"):
    m, k = a.shape
    n = w.shape[1]
    tn = min(tn, n)
    tk = k // k_steps
    assert k_steps == 1 or out_dtype == F32
    return pl.pallas_call(
        functools.partial(_matmul_kernel, k_steps=k_steps),
        out_shape=jax.ShapeDtypeStruct((m, n), out_dtype),
        grid=(m // tm, n // tn, k_steps),
        in_specs=[pl.BlockSpec((tm, tk), lambda i, j, kk: (i, kk)),
                  pl.BlockSpec((tk, tn), lambda i, j, kk: (kk, j))],
        out_specs=pl.BlockSpec((tm, tn), lambda i, j, kk: (i, j)),
        compiler_params=_cparams(("parallel", "arbitrary", "arbitrary")),
        name=name,
    )(a, w)


def _matmul_concat_kernel(a1_ref, a2_ref, w_ref, o_ref, *, steps1):
    kk = pl.program_id(2)

    @pl.when(kk == 0)
    def _():
        o_ref[...] = _dot(a1_ref[...], w_ref[...])

    @pl.when((kk > 0) & (kk < steps1))
    def _():
        o_ref[...] += _dot(a1_ref[...], w_ref[...])

    @pl.when(kk >= steps1)
    def _():
        o_ref[...] += _dot(a2_ref[...], w_ref[...])


def _matmul_concat(a1, a2, w, *, tk, tm=1024, tn=512, name="matmul_concat"):
    m, k1 = a1.shape
    k2 = a2.shape[1]
    n = w.shape[1]
    steps1, steps2 = k1 // tk, k2 // tk
    assert steps1 * tk == k1 and steps2 * tk == k2 and w.shape[0] == k1 + k2
    return pl.pallas_call(
        functools.partial(_matmul_concat_kernel, steps1=steps1),
        out_shape=jax.ShapeDtypeStruct((m, n), F32),
        grid=(m // tm, n // tn, steps1 + steps2),
        in_specs=[pl.BlockSpec((tm, tk), lambda i, j, kk: (i, jnp.minimum(kk, steps1 - 1))),
                  pl.BlockSpec((tm, tk), lambda i, j, kk: (i, jnp.maximum(kk - steps1, 0))),
                  pl.BlockSpec((tk, tn), lambda i, j, kk: (kk, j))],
        out_specs=pl.BlockSpec((tm, tn), lambda i, j, kk: (i, j)),
        compiler_params=_cparams(("parallel", "arbitrary", "arbitrary")),
        name=name,
    )(a1, a2, w)


def _gateup_kernel(a_ref, wg_ref, wu_ref, o_ref):
    a = a_ref[...]
    g = _dot(a, wg_ref[...].astype(BF16))
    u = _dot(a, wu_ref[...].astype(BF16))
    o_ref[...] = (_silu(g) * u).astype(o_ref.dtype)


def _gateup(a, wg, wu, *, tm=2048, tn=256):
    m, k = a.shape
    n = wg.shape[1]
    return pl.pallas_call(
        _gateup_kernel,
        out_shape=jax.ShapeDtypeStruct((m, n), BF16),
        grid=(m // tm, n // tn),
        in_specs=[pl.BlockSpec((tm, k), lambda i, j: (i, 0), pipeline_mode=pl.Buffered(1)),
                  pl.BlockSpec((k, tn), lambda i, j: (0, j)),
                  pl.BlockSpec((k, tn), lambda i, j: (0, j))],
        out_specs=pl.BlockSpec((tm, tn), lambda i, j: (i, j)),
        compiler_params=_cparams(("parallel", "arbitrary")),
        name="swiglu_gateup",
    )(a, wg, wu)


def _postnorm_kernel(f_ref, h_ref, nw_ref, next_w_ref, o_ref, xn_ref, *, scale):
    def body(rows):
        f = f_ref[rows, :]
        hn = h_ref[rows, :] + scale * (f * _rms_scale(f) * nw_ref[...])
        o_ref[rows, :] = hn
        xn_ref[rows, :] = (hn * _rms_scale(hn) * next_w_ref[...]).astype(xn_ref.dtype)

    _row_slabs(f_ref.shape[0], 16, body, unroll=4)


def _postnorm(f, h, nw, scale, next_w, *, tm=256):
    m, n = f.shape
    row_spec = pl.BlockSpec((tm, n), lambda i: (i, 0))
    vec_spec = pl.BlockSpec((1, n), lambda i: (0, 0))
    return pl.pallas_call(
        functools.partial(_postnorm_kernel, scale=scale),
        out_shape=(jax.ShapeDtypeStruct((m, n), F32), jax.ShapeDtypeStruct((m, n), BF16)),
        grid=(m // tm,),
        in_specs=[row_spec, row_spec, vec_spec, vec_spec],
        out_specs=(row_spec, row_spec),
        compiler_params=_cparams(("parallel",)),
        name="postnorm_residual",
    )(f, h, nw.reshape(1, n), next_w.reshape(1, n))


def _ple_kernel(g_ref, p_ref, wp_ref, h_ref, nw_ref, o_ref):
    def body(rows):
        proj = _dot(p_ref[rows, :].astype(BF16), wp_ref[...])
        f = jax.nn.sigmoid(g_ref[rows, :]) * proj
        o_ref[rows, :] = h_ref[rows, :] + f * _rms_scale(f) * nw_ref[...]

    _row_slabs(g_ref.shape[0], 64, body)


def _ple(g, p, wp, h, nw, *, tm=256):
    m, n = g.shape
    pdim = p.shape[1]
    row_spec = pl.BlockSpec((tm, n), lambda i: (i, 0))
    return pl.pallas_call(
        _ple_kernel,
        out_shape=jax.ShapeDtypeStruct((m, n), F32),
        grid=(m // tm,),
        in_specs=[row_spec,
                  pl.BlockSpec((tm, pdim), lambda i: (i, 0)),
                  pl.BlockSpec((pdim, n), lambda i: (0, 0)),
                  row_spec,
                  pl.BlockSpec((1, n), lambda i: (0, 0))],
        out_specs=row_spec,
        compiler_params=_cparams(("parallel",)),
        name="ple_gate_postnorm",
    )(g, p, wp, h, nw.reshape(1, n))


def _ssd_kernel(z_ref, x_ref, bc_ref, dt_ref, cwx_ref, cbx_ref, cwbc_ref, cbbc_ref, dtb_ref, alog_ref,
                dskip_ref, nw_ref, head_to_feat_ref, head_to_tile_ref, o_ref, xbuf, bcbuf, state):
    c = pl.program_id(1)
    q = SSD_CHUNK
    halo = CONV_HALO_ROWS

    @pl.when(c == 0)
    def _():
        xbuf[0:halo, :] = jnp.zeros((halo, xbuf.shape[1]), F32)
        bcbuf[0:halo, :] = jnp.zeros((halo, bcbuf.shape[1]), F32)
        state[...] = jnp.zeros(state.shape, F32)

    xbuf[halo:halo + q, :] = x_ref[...].astype(F32)
    bcbuf[halo:halo + q, :] = bc_ref[...].astype(F32)

    v = dt_ref[...] + dtb_ref[...]
    dt = jnp.maximum(v, 0.0) + jnp.log(1.0 + jnp.exp(-jnp.abs(v)))
    da = dt * -jnp.exp(alog_ref[...])
    row = lax.broadcasted_iota(jnp.int32, (q, q), 0)
    col = lax.broadcasted_iota(jnp.int32, (q, q), 1)
    causal = row >= col
    tril = jnp.where(causal, 1.0, 0.0).astype(BF16)
    da_hi, da_lo = _split_bf16(da)
    a_cs = _dot(tril, da_hi) + _dot(tril, da_lo)
    a_cs_t = a_cs.T
    head_to_feat = head_to_feat_ref[...]
    a_feat = _dot_split_lhs(a_cs, head_to_feat)
    dt_feat = _dot_split_lhs(dt, head_to_feat)
    a_tile = _dot_split_lhs(a_cs, head_to_tile_ref[...])

    lane = lax.broadcasted_iota(jnp.int32, (q, LANES), 1)
    heads_per_tile = LANES // SSD_HEAD_DIM

    for g in range(SSD_GROUPS):
        lo, hi = g * SSD_GROUP_PAD, (g + 1) * SSD_GROUP_PAD
        blo, clo = g * SSD_STATE, (SSD_GROUPS + g) * SSD_STATE

        def conv(buf, w_ref, b_ref, c0, c1):
            acc = b_ref[:, c0:c1] + w_ref[0:1, c0:c1] * buf[pl.ds(halo - 3, q), c0:c1]
            for t in range(1, SSD_CONV):
                acc = acc + w_ref[t:t + 1, c0:c1] * buf[pl.ds(halo - 3 + t, q), c0:c1]
            return _silu(acc)

        xg = conv(xbuf, cwx_ref, cbx_ref, lo, hi)
        bg = conv(bcbuf, cwbc_ref, cbbc_ref, blo, blo + SSD_STATE)
        cg = conv(bcbuf, cwbc_ref, cbbc_ref, clo, clo + SSD_STATE)
        bg16 = bg.astype(BF16)
        cg16 = cg.astype(BF16)

        a_g = a_feat[:, lo:hi]
        a_last = a_g[q - 1:q, :]
        xdt = xg * dt_feat[:, lo:hi]

        cb = _dot_nt(cg16, bg16)
        m_parts = []
        for jj in range(SSD_HEADS_PER_GROUP):
            j = g * SSD_HEADS_PER_GROUP + jj
            seg = a_tile[:, j * q:(j + 1) * q] - a_cs_t[j:j + 1, :]
            decay = jnp.exp(jnp.where(causal, seg, -jnp.inf))
            m_parts.append((cb * decay).astype(BF16))
        y_tiles = []
        for t in range(SSD_GROUP_PAD // LANES):
            xt = xdt[:, t * LANES:(t + 1) * LANES]
            heads = [jj for jj in range(t * heads_per_tile, (t + 1) * heads_per_tile) if jj < SSD_HEADS_PER_GROUP]
            if len(heads) == 1:
                y_tiles.append(_dot(m_parts[heads[0]], xt.astype(BF16)))
            else:
                xs = [jnp.where(lane // SSD_HEAD_DIM == jj % heads_per_tile, xt, 0.0).astype(BF16) for jj in heads]
                y_tiles.append(_dot(jnp.concatenate([m_parts[jj] for jj in heads], axis=1),
                                    jnp.concatenate(xs, axis=0)))
        y = jnp.concatenate(y_tiles, axis=1)

        prev = state[:, lo:hi]
        y = y + _dot(cg16, prev.astype(BF16)) * jnp.exp(a_g)
        xdd = (xdt * jnp.exp(a_last - a_g)).astype(BF16)
        state[:, lo:hi] = prev * jnp.exp(a_last) + _dot_tn(bg16, xdd)

        y = y + xg * dskip_ref[:, lo:hi]
        y = y * _silu(z_ref[:, lo:hi].astype(F32))
        ms = jnp.sum(y * y, axis=-1, keepdims=True) * (1.0 / SSD_GROUP_WIDTH)
        o_ref[:, lo:hi] = (y * lax.rsqrt(ms + RMS_EPS) * nw_ref[:, lo:hi]).astype(o_ref.dtype)

    xbuf[0:halo, :] = xbuf[q:q + halo, :]
    bcbuf[0:halo, :] = bcbuf[q:q + halo, :]


def _ssd(zx, bc, dt, consts, *, batch, seq):
    (cwx, cbx, cwbc, cbbc, dtb, alog, dskip, nw, head_to_feat, head_to_tile) = consts
    q = SSD_CHUNK
    nc = seq // q
    m = batch * seq
    wp = SSD_WIDTH_PAD

    def tok(b, c):
        return b * nc + c

    def const_spec(arr):
        return pl.BlockSpec(arr.shape, lambda b, c: (0, 0))

    return pl.pallas_call(
        _ssd_kernel,
        out_shape=jax.ShapeDtypeStruct((m, wp), BF16),
        grid=(batch, nc),
        in_specs=[pl.BlockSpec((q, wp), lambda b, c: (tok(b, c), 0)),
                  pl.BlockSpec((q, wp), lambda b, c: (tok(b, c), 1)),
                  pl.BlockSpec((q, SSD_BC_WIDTH), lambda b, c: (tok(b, c), 0)),
                  pl.BlockSpec((q, LANES), lambda b, c: (tok(b, c), 0)),
                  const_spec(cwx), const_spec(cbx), const_spec(cwbc), const_spec(cbbc),
                  const_spec(dtb), const_spec(alog), const_spec(dskip), const_spec(nw),
                  const_spec(head_to_feat), const_spec(head_to_tile)],
        out_specs=pl.BlockSpec((q, wp), lambda b, c: (tok(b, c), 0)),
        scratch_shapes=[pltpu.VMEM((q + CONV_HALO_ROWS, wp), F32),
                        pltpu.VMEM((q + CONV_HALO_ROWS, SSD_BC_WIDTH), F32),
                        pltpu.VMEM((SSD_STATE, wp), F32)],
        compiler_params=_cparams(("parallel", "arbitrary")),
        name="ssd_chunk_scan",
    )(zx, zx, bc, dt, cwx, cbx, cwbc, cbbc, dtb, alog, dskip, nw, head_to_feat, head_to_tile)


def _alibi_slopes(n):
    def pow2(m):
        start = 2.0 ** (-8.0 / m)
        return [start ** (i + 1) for i in range(m)]
    if math.log2(n).is_integer():
        s = pow2(n)
    else:
        c = 2 ** int(math.floor(math.log2(n)))
        s = pow2(c) + pow2(2 * c)[0::2][: n - c]
    return [float(np.float32(v)) for v in s]


def _alibi_base(n_query, n_key, key_offset):
    qi = lax.broadcasted_iota(jnp.int32, (n_query, n_key), 0)
    kj = lax.broadcasted_iota(jnp.int32, (n_query, n_key), 1)
    dist = qi - kj + key_offset
    return jnp.where((dist >= 0) & (dist <= ATT_BLOCK), -dist.astype(F32), -jnp.inf)


def _attn_kernel(q_ref, k_ref, v_ref, slope_ref, o_ref, o_acc, lse_acc):
    seq = q_ref.shape[0]
    c = ATT_BLOCK

    def attend(g, d, start, qb, bases):
        stride = None if d == 1 else d
        rows = pl.ds(start, qb, stride=stride)
        q = q_ref[rows, :].astype(BF16)
        tok_slope = slope_ref[0][:, :qb] * float(d)
        s_cur = _dot_nt(q, k_ref[rows, :].astype(BF16)) + tok_slope * bases[0]
        mx = jnp.max(s_cur, axis=-1, keepdims=True)
        if bases[1] is not None:
            prows = pl.ds(start - c * d, c, stride=stride)
            s_prev = _dot_nt(q, k_ref[prows, :].astype(BF16)) + tok_slope[:, :c] * bases[1]
            mx = jnp.maximum(mx, jnp.max(s_prev, axis=-1, keepdims=True))
        p_cur = jnp.exp(s_cur - mx)
        den = jnp.sum(p_cur, axis=-1, keepdims=True)
        o = _dot(p_cur.astype(BF16), v_ref[rows, :].astype(BF16))
        if bases[1] is not None:
            p_prev = jnp.exp(s_prev - mx)
            den = den + jnp.sum(p_prev, axis=-1, keepdims=True)
            o = o + _dot(p_prev.astype(BF16), v_ref[prows, :].astype(BF16))
        o_acc[g, rows, :] = o / den
        lse_acc[g, rows, :] = jnp.broadcast_to(mx + jnp.log(den), (qb, LANES))

    for g, (_, d) in enumerate(ATT_BRANCHES):
        sub_len = seq // d
        qb = min(ATT_QUERY_BLOCK, sub_len)
        n_blocks = sub_len // qb
        base_cur = _alibi_base(qb, qb, 0)
        base_prev = _alibi_base(qb, c, c) if n_blocks > 1 else None

        def residue(r, carry, g=g, d=d, qb=qb, n_blocks=n_blocks, base_cur=base_cur, base_prev=base_prev):
            attend(g, d, r, qb, (base_cur, None))

            def later(nb, inner):
                offset = nb * (qb * d)
                attend(g, d, r + (pl.multiple_of(offset, qb) if d == 1 else offset), qb, (base_cur, base_prev))
                return inner

            if n_blocks > 1:
                lax.fori_loop(1, n_blocks, later, 0)
            return carry

        if d == 1:
            residue(0, 0)
        else:
            lax.fori_loop(0, d, residue, 0, unroll=2 if n_blocks == 1 else 1)

    def merge(i, carry):
        rows = pl.ds(pl.multiple_of(i * ATT_MERGE_ROWS, ATT_MERGE_ROWS), ATT_MERGE_ROWS)
        lses = [lse_acc[g, rows, :] for g in range(len(ATT_BRANCHES))]
        mx = functools.reduce(jnp.maximum, lses)
        es = [jnp.exp(l - mx) for l in lses]
        num = sum(e * o_acc[g, rows, :] for g, e in enumerate(es))
        o_ref[rows, :] = (num / sum(es)).astype(o_ref.dtype)
        return carry

    lax.fori_loop(0, seq // ATT_MERGE_ROWS, merge, 0)


def _dilated_attention(qkv, *, batch, seq):
    e = ATT_HEAD_DIM
    n_branch = len(ATT_BRANCHES)
    assert all(seq % (d * ATT_BLOCK) == 0 for _, d in ATT_BRANCHES)
    slopes = np.repeat(np.asarray(_alibi_slopes(ATT_HEADS), np.float32)[:, None, None], ATT_QUERY_BLOCK, axis=2)

    def head_spec(part):
        return pl.BlockSpec((seq, e), lambda b, h: (b, part * ATT_HEADS + h))

    return pl.pallas_call(
        _attn_kernel,
        out_shape=jax.ShapeDtypeStruct((batch * seq, ATT_WIDTH), BF16),
        grid=(batch, ATT_HEADS),
        in_specs=[head_spec(0), head_spec(1), head_spec(2),
                  pl.BlockSpec((1, 1, ATT_QUERY_BLOCK), lambda b, h: (h, 0, 0))],
        out_specs=pl.BlockSpec((seq, e), lambda b, h: (b, h)),
        scratch_shapes=[pltpu.VMEM((n_branch, seq, e), F32),
                        pltpu.VMEM((n_branch, seq, LANES), F32)],
        compiler_params=_cparams(("parallel", "parallel")),
        name="dilated_attention",
    )(qkv, qkv, qkv, jnp.asarray(slopes))


def _pad_groups(t):
    lead = t.shape[:-1]
    t = t.reshape(*lead, SSD_GROUPS, SSD_GROUP_WIDTH)
    t = jnp.pad(t, [(0, 0)] * len(lead) + [(0, 0), (0, SSD_GROUP_PAD - SSD_GROUP_WIDTH)])
    return t.reshape(*lead, SSD_WIDTH_PAD)


def _pad_last(t, n):
    return jnp.pad(t, [(0, 0)] * (t.ndim - 1) + [(0, n - t.shape[-1])])


def _head_selectors():
    head_to_feat = np.zeros((LANES, SSD_WIDTH_PAD), np.float32)
    head_to_tile = np.zeros((LANES, SSD_HEADS * SSD_CHUNK), np.float32)
    for j in range(SSD_HEADS):
        g, jj = divmod(j, SSD_HEADS_PER_GROUP)
        f0 = g * SSD_GROUP_PAD + jj * SSD_HEAD_DIM
        head_to_feat[j, f0:f0 + SSD_HEAD_DIM] = 1.0
        head_to_tile[j, j * SSD_CHUNK:(j + 1) * SSD_CHUNK] = 1.0
    return jnp.asarray(head_to_feat, BF16), jnp.asarray(head_to_tile, BF16)


def _ffn(h, xn, post_w, w_gate, w_up, w_down, next_pre_w):
    act = _gateup(xn, w_gate, w_up)
    f = _matmul(act, w_down.astype(BF16), F32, k_steps=2, name="ffn_down")
    return _postnorm(f, h, post_w, 0.5, next_pre_w)


def kernel(x, p, ffn1_pre_w, ffn1_post_w, ffn1_w_gate, ffn1_w_up, ffn1_w_down, mix_pre_w, mix_post_w, w_in, conv_w, conv_b, dt_bias, a_log, d_skip, ssd_norm_w, w_out, ffn2_pre_w, ffn2_post_w, ffn2_w_gate, ffn2_w_up, ffn2_w_down, ple_pre_w, ple_post_w, w_ple_gate, w_ple_proj):
    batch, seq, d_model = x.shape
    depth = w_in.shape[0]
    m = batch * seq
    h = x.reshape(m, d_model)
    head_to_feat, head_to_tile = _head_selectors()
    i1 = SSD_WIDTH
    i2 = i1 + SSD_WIDTH
    i3 = i2 + SSD_BC_WIDTH
    i4 = i3 + SSD_HEADS

    for i in range(depth):
        xn = _rmsnorm(h, ffn1_pre_w[i])
        h, u = _ffn(h, xn, ffn1_post_w[i], ffn1_w_gate[i], ffn1_w_up[i], ffn1_w_down[i], mix_pre_w[i])

        wi = w_in[i]
        w_zx = jnp.concatenate([_pad_groups(wi[:, :i1]), _pad_groups(wi[:, i1:i2])], axis=1).astype(BF16)
        w_bc = wi[:, i2:i3].astype(BF16)
        w_dt = _pad_last(wi[:, i3:i4], LANES).astype(BF16)
        i5 = i4 + ATT_WIDTH
        w_qkv = jnp.concatenate([wi[:, i4:i5] * (ATT_HEAD_DIM ** -0.5), wi[:, i5:]], axis=1).astype(BF16)
        zx = _matmul(u, w_zx, BF16, name="in_proj_zx")
        bc = _matmul(u, w_bc, BF16, name="in_proj_bc")
        dt = _matmul(u, w_dt, F32, name="in_proj_dt")
        qkv = _matmul(u, w_qkv, F32, name="in_proj_qkv")

        cw, cb = conv_w[i], conv_b[i]
        ssd_consts = (
            _pad_groups(cw[:, :SSD_WIDTH]), _pad_groups(cb[None, :SSD_WIDTH]),
            cw[:, SSD_WIDTH:], cb[None, SSD_WIDTH:],
            _pad_last(dt_bias[i][None, :], LANES),
            _pad_last(a_log[i][None, :], LANES),
            _pad_groups(jnp.repeat(d_skip[i], SSD_HEAD_DIM)[None, :]),
            _pad_groups(ssd_norm_w[i][None, :]),
            head_to_feat, head_to_tile)
        y_ssd = _ssd(zx, bc, dt, ssd_consts, batch=batch, seq=seq)
        att = _dilated_attention(qkv, batch=batch, seq=seq)

        wo = w_out[i]
        wo_ssd = jnp.pad(wo[:SSD_WIDTH].reshape(SSD_GROUPS, SSD_GROUP_WIDTH, d_model),
                         ((0, 0), (0, SSD_GROUP_PAD - SSD_GROUP_WIDTH), (0, 0))).reshape(SSD_WIDTH_PAD, d_model)
        wo_pad = jnp.concatenate([wo_ssd, wo[SSD_WIDTH:]], axis=0).astype(BF16)
        mix = _matmul_concat(y_ssd, att, wo_pad, tk=ATT_WIDTH, name="out_proj")
        h, xn = _postnorm(mix, h, mix_post_w[i], 1.0, ffn2_pre_w[i])

        h, xn = _ffn(h, xn, ffn2_post_w[i], ffn2_w_gate[i], ffn2_w_up[i], ffn2_w_down[i], ple_pre_w[i])

        gate = _matmul(xn, w_ple_gate[i].astype(BF16), F32, name="ple_gate")
        h = _ple(gate, p[i].reshape(m, -1), w_ple_proj[i].astype(BF16), h, ple_post_w[i])

    return h.reshape(batch, seq, d_model)
```

```python
import functools
import math

import numpy as np
import jax
import jax.numpy as jnp
from jax import lax
from jax.experimental import pallas as pl
from jax.experimental.pallas import tpu as pltpu

F32 = jnp.float32
BF16 = jnp.bfloat16

RMS_EPS = 1e-6

LANES = 128
SUBLANES = 8
VMEM_LIMIT_BYTES = 58 * 1024 * 1024

SSD_HEAD_DIM = 64
SSD_HEADS = 40
SSD_GROUPS = 8
SSD_HEADS_PER_GROUP = SSD_HEADS // SSD_GROUPS
SSD_GROUP_WIDTH = SSD_HEADS_PER_GROUP * SSD_HEAD_DIM
SSD_GROUP_PAD = 384
SSD_WIDTH = SSD_HEADS * SSD_HEAD_DIM
SSD_WIDTH_PAD = SSD_GROUPS * SSD_GROUP_PAD
SSD_STATE = 128
SSD_CONV = 4
SSD_CHUNK = 128
SSD_BC_WIDTH = 2 * SSD_GROUPS * SSD_STATE
CONV_HALO_ROWS = SUBLANES

ATT_HEAD_DIM = 128
ATT_HEADS = 12
ATT_WIDTH = ATT_HEADS * ATT_HEAD_DIM
ATT_BRANCHES = ((128, 1), (512, 4), (2048, 16))
ATT_BLOCK = 128
ATT_QUERY_BLOCK = 512
ATT_MERGE_ROWS = 256


def _cparams(semantics):
    return pltpu.CompilerParams(dimension_semantics=semantics, vmem_limit_bytes=VMEM_LIMIT_BYTES)


def _dot(a, b):
    return jnp.dot(a, b, preferred_element_type=F32)


def _dot_nt(a, b):
    return lax.dot_general(a, b, (((1,), (1,)), ((), ())), preferred_element_type=F32)


def _dot_tn(a, b):
    return lax.dot_general(a, b, (((0,), (0,)), ((), ())), preferred_element_type=F32)


def _split_bf16(x):
    hi = x.astype(BF16)
    lo = (x - hi.astype(F32)).astype(BF16)
    return hi, lo


def _dot_split_lhs(x, sel):
    hi, lo = _split_bf16(x)
    return _dot(hi, sel) + _dot(lo, sel)


def _silu(x):
    hx = 0.5 * x
    return hx + hx * jnp.tanh(hx)


def _rms_scale(v):
    return lax.rsqrt(jnp.mean(v * v, axis=-1, keepdims=True) + RMS_EPS)


def _row_slabs(n_rows, rows_per_step, body, unroll=1):
    def step(r, carry):
        body(pl.ds(pl.multiple_of(r * rows_per_step, rows_per_step), rows_per_step))
        return carry

    lax.fori_loop(0, n_rows // rows_per_step, step, 0, unroll=unroll)


def _rmsnorm_kernel(x_ref, w_ref, o_ref):
    def body(rows):
        x = x_ref[rows, :]
        o_ref[rows, :] = (x * _rms_scale(x) * w_ref[...]).astype(o_ref.dtype)

    _row_slabs(x_ref.shape[0], 16, body, unroll=4)


def _rmsnorm(x, w, *, tm=256):
    m, d = x.shape
    return pl.pallas_call(
        _rmsnorm_kernel,
        out_shape=jax.ShapeDtypeStruct((m, d), BF16),
        grid=(m // tm,),
        in_specs=[pl.BlockSpec((tm, d), lambda i: (i, 0)),
                  pl.BlockSpec((1, d), lambda i: (0, 0))],
        out_specs=pl.BlockSpec((tm, d), lambda i: (i, 0)),
        compiler_params=_cparams(("parallel",)),
        name="rmsnorm",
    )(x, w.reshape(1, d))


def _matmul_kernel(a_ref, w_ref, o_ref, *, k_steps):
    part = _dot(a_ref[...], w_ref[...].astype(BF16))
    if k_steps == 1:
        o_ref[...] = part.astype(o_ref.dtype)
    else:
        k = pl.program_id(2)

        @pl.when(k == 0)
        def _():
            o_ref[...] = part

        @pl.when(k > 0)
        def _():
            o_ref[...] += part


def _matmul(a, w, out_dtype, *, tm=1024, tn=512, k_steps=1, w_cols=None, name="matmul"):
    m, k = a.shape
    start, n = (0, w.shape[1]) if w_cols is None else w_cols
    tn = min(tn, n)
    tk = k // k_steps
    assert k_steps == 1 or out_dtype == F32
    assert start % tn == 0 and n % tn == 0
    first = start // tn
    return pl.pallas_call(
        functools.partial(_matmul_kernel, k_steps=k_steps),
        out_shape=jax.ShapeDtypeStruct((m, n), out_dtype),
        grid=(m // tm, n // tn, k_steps),
        in_specs=[pl.BlockSpec((tm, tk), lambda i, j, kk: (i, kk)),
                  pl.BlockSpec((tk, tn), lambda i, j, kk: (kk, first + j))],
        out_specs=pl.BlockSpec((tm, tn), lambda i, j, kk: (i, j)),
        compiler_params=_cparams(("parallel", "arbitrary", "arbitrary")),
        name=name,
    )(a, w)


def _matmul_pad_groups_kernel(a_ref, w_ref, o_ref, *, groups_per_tile):
    res = _dot(a_ref[...], w_ref[...].astype(BF16))
    pad = jnp.zeros((res.shape[0], SSD_GROUP_PAD - SSD_GROUP_WIDTH), F32)
    parts = []
    for g in range(groups_per_tile):
        parts += [res[:, g * SSD_GROUP_WIDTH:(g + 1) * SSD_GROUP_WIDTH], pad]
    o_ref[...] = jnp.concatenate(parts, axis=1).astype(o_ref.dtype)


def _matmul_pad_groups(a, w, *, tm=1024, groups_per_tile=4, name="matmul_pad_groups"):
    m, k = a.shape
    n = w.shape[1]
    tn_in = groups_per_tile * SSD_GROUP_WIDTH
    tn_out = groups_per_tile * SSD_GROUP_PAD
    assert n % tn_in == 0 and tn_in % LANES == 0
    return pl.pallas_call(
        functools.partial(_matmul_pad_groups_kernel, groups_per_tile=groups_per_tile),
        out_shape=jax.ShapeDtypeStruct((m, n // tn_in * tn_out), BF16),
        grid=(m // tm, n // tn_in),
        in_specs=[pl.BlockSpec((tm, k), lambda i, j: (i, 0)),
                  pl.BlockSpec((k, tn_in), lambda i, j: (0, j))],
        out_specs=pl.BlockSpec((tm, tn_out), lambda i, j: (i, j)),
        compiler_params=_cparams(("parallel", "arbitrary")),
        name=name,
    )(a, w)


def _gateup_kernel(a_ref, wg_ref, wu_ref, o_ref):
    a = a_ref[...]
    g = _dot(a, wg_ref[...].astype(BF16))
    u = _dot(a, wu_ref[...].astype(BF16))
    o_ref[...] = (_silu(g) * u).astype(o_ref.dtype)


def _gateup(a, wg, wu, *, tm=2048, tn=256):
    m, k = a.shape
    n = wg.shape[1]
    return pl.pallas_call(
        _gateup_kernel,
        out_shape=jax.ShapeDtypeStruct((m, n), BF16),
        grid=(m // tm, n // tn),
        in_specs=[pl.BlockSpec((tm, k), lambda i, j: (i, 0), pipeline_mode=pl.Buffered(1)),
                  pl.BlockSpec((k, tn), lambda i, j: (0, j)),
                  pl.BlockSpec((k, tn), lambda i, j: (0, j))],
        out_specs=pl.BlockSpec((tm, tn), lambda i, j: (i, j)),
        compiler_params=_cparams(("parallel", "arbitrary")),
        name="swiglu_gateup",
    )(a, wg, wu)


def _postnorm_kernel(f_ref, h_ref, nw_ref, next_w_ref, o_ref, xn_ref, *, scale):
    def body(rows):
        f = f_ref[rows, :]
        hn = h_ref[rows, :] + scale * (f * _rms_scale(f) * nw_ref[...])
        o_ref[rows, :] = hn
        xn_ref[rows, :] = (hn * _rms_scale(hn) * next_w_ref[...]).astype(xn_ref.dtype)

    _row_slabs(f_ref.shape[0], 16, body, unroll=4)


def _postnorm(f, h, nw, scale, next_w, *, tm=256):
    m, n = f.shape
    row_spec = pl.BlockSpec((tm, n), lambda i: (i, 0))
    vec_spec = pl.BlockSpec((1, n), lambda i: (0, 0))
    return pl.pallas_call(
        functools.partial(_postnorm_kernel, scale=scale),
        out_shape=(jax.ShapeDtypeStruct((m, n), F32), jax.ShapeDtypeStruct((m, n), BF16)),
        grid=(m // tm,),
        in_specs=[row_spec, row_spec, vec_spec, vec_spec],
        out_specs=(row_spec, row_spec),
        compiler_params=_cparams(("parallel",)),
        name="postnorm_residual",
    )(f, h, nw.reshape(1, n), next_w.reshape(1, n))


def _ple_kernel(g_ref, p_ref, wp_ref, h_ref, nw_ref, o_ref):
    def body(rows):
        proj = _dot(p_ref[rows, :].astype(BF16), wp_ref[...])
        f = jax.nn.sigmoid(g_ref[rows, :]) * proj
        o_ref[rows, :] = h_ref[rows, :] + f * _rms_scale(f) * nw_ref[...]

    _row_slabs(g_ref.shape[0], 64, body)


def _ple(g, p, wp, h, nw, *, tm=256):
    m, n = g.shape
    pdim = p.shape[1]
    row_spec = pl.BlockSpec((tm, n), lambda i: (i, 0))
    return pl.pallas_call(
        _ple_kernel,
        out_shape=jax.ShapeDtypeStruct((m, n), F32),
        grid=(m // tm,),
        in_specs=[row_spec,
                  pl.BlockSpec((tm, pdim), lambda i: (i, 0)),
                  pl.BlockSpec((pdim, n), lambda i: (0, 0)),
                  row_spec,
                  pl.BlockSpec((1, n), lambda i: (0, 0))],
        out_specs=row_spec,
        compiler_params=_cparams(("parallel",)),
        name="ple_gate_postnorm",
    )(g, p, wp, h, nw.reshape(1, n))


def _ssd_kernel(z_ref, x_ref, bc_ref, dt_ref, cwx_ref, cbx_ref, cwbc_ref, cbbc_ref, dtb_ref, alog_ref,
                dskip_ref, nw_ref, head_to_feat_ref, head_to_tile_ref, o_ref, xbuf, bcbuf, state):
    c = pl.program_id(1)
    q = SSD_CHUNK
    halo = CONV_HALO_ROWS

    @pl.when(c == 0)
    def _():
        xbuf[0:halo, :] = jnp.zeros((halo, xbuf.shape[1]), F32)
        bcbuf[0:halo, :] = jnp.zeros((halo, bcbuf.shape[1]), F32)
        state[...] = jnp.zeros(state.shape, F32)

    xbuf[halo:halo + q, :] = x_ref[...].astype(F32)
    bcbuf[halo:halo + q, :] = bc_ref[...].astype(F32)

    v = dt_ref[...] + dtb_ref[...]
    dt = jnp.maximum(v, 0.0) + jnp.log(1.0 + jnp.exp(-jnp.abs(v)))
    head_lane = lax.broadcasted_iota(jnp.int32, dt.shape, 1)
    dt = jnp.where(head_lane < SSD_HEADS, dt, 0.0)
    da = dt * -jnp.exp(alog_ref[...])
    row = lax.broadcasted_iota(jnp.int32, (q, q), 0)
    col = lax.broadcasted_iota(jnp.int32, (q, q), 1)
    causal = row >= col
    tril = jnp.where(causal, 1.0, 0.0).astype(BF16)
    da_hi, da_lo = _split_bf16(da)
    a_cs = _dot(tril, da_hi) + _dot(tril, da_lo)
    a_cs_t = a_cs.T
    head_to_feat = head_to_feat_ref[...]
    a_feat = _dot_split_lhs(a_cs, head_to_feat)
    dt_feat = _dot_split_lhs(dt, head_to_feat)
    a_tile = _dot_split_lhs(a_cs, head_to_tile_ref[...])

    lane = lax.broadcasted_iota(jnp.int32, (q, LANES), 1)
    heads_per_tile = LANES // SSD_HEAD_DIM

    for g in range(SSD_GROUPS):
        lo, hi = g * SSD_GROUP_PAD, (g + 1) * SSD_GROUP_PAD
        blo, clo = g * SSD_STATE, (SSD_GROUPS + g) * SSD_STATE

        def conv(buf, w_ref, b_ref, c0, c1):
            acc = b_ref[:, c0:c1] + w_ref[0:1, c0:c1] * buf[pl.ds(halo - 3, q), c0:c1]
            for t in range(1, SSD_CONV):
                acc = acc + w_ref[t:t + 1, c0:c1] * buf[pl.ds(halo - 3 + t, q), c0:c1]
            return _silu(acc)

        xg = conv(xbuf, cwx_ref, cbx_ref, lo, hi)
        bg = conv(bcbuf, cwbc_ref, cbbc_ref, blo, blo + SSD_STATE)
        cg = conv(bcbuf, cwbc_ref, cbbc_ref, clo, clo + SSD_STATE)
        bg16 = bg.astype(BF16)
        cg16 = cg.astype(BF16)

        a_g = a_feat[:, lo:hi]
        a_last = a_g[q - 1:q, :]
        xdt = xg * dt_feat[:, lo:hi]

        cb = _dot_nt(cg16, bg16)
        m_parts = []
        for jj in range(SSD_HEADS_PER_GROUP):
            j = g * SSD_HEADS_PER_GROUP + jj
            seg = a_tile[:, j * q:(j + 1) * q] - a_cs_t[j:j + 1, :]
            decay = jnp.exp(jnp.where(causal, seg, -jnp.inf))
            m_parts.append((cb * decay).astype(BF16))
        y_tiles = []
        for t in range(SSD_GROUP_PAD // LANES):
            xt = xdt[:, t * LANES:(t + 1) * LANES]
            heads = [jj for jj in range(t * heads_per_tile, (t + 1) * heads_per_tile) if jj < SSD_HEADS_PER_GROUP]
            if len(heads) == 1:
                y_tiles.append(_dot(m_parts[heads[0]], xt.astype(BF16)))
            else:
                xs = [jnp.where(lane // SSD_HEAD_DIM == jj % heads_per_tile, xt, 0.0).astype(BF16) for jj in heads]
                y_tiles.append(_dot(jnp.concatenate([m_parts[jj] for jj in heads], axis=1),
                                    jnp.concatenate(xs, axis=0)))
        y = jnp.concatenate(y_tiles, axis=1)

        prev = state[:, lo:hi]
        y = y + _dot(cg16, prev.astype(BF16)) * jnp.exp(a_g)
        xdd = (xdt * jnp.exp(a_last - a_g)).astype(BF16)
        state[:, lo:hi] = prev * jnp.exp(a_last) + _dot_tn(bg16, xdd)

        y = y + xg * dskip_ref[:, lo:hi]
        y = y * _silu(z_ref[:, lo:hi].astype(F32))
        ms = jnp.sum(y * y, axis=-1, keepdims=True) * (1.0 / SSD_GROUP_WIDTH)
        y = y * lax.rsqrt(ms + RMS_EPS) * nw_ref[:, lo:hi]
        o_ref[:, g * SSD_GROUP_WIDTH:(g + 1) * SSD_GROUP_WIDTH] = y[:, :SSD_GROUP_WIDTH].astype(o_ref.dtype)

    xbuf[0:halo, :] = xbuf[q:q + halo, :]
    bcbuf[0:halo, :] = bcbuf[q:q + halo, :]


def _ssd(zx, bc, dt, consts, *, batch, seq, out_width):
    (cwx, cbx, cwbc, cbbc, dtb, alog, dskip, nw, head_to_feat, head_to_tile) = consts
    q = SSD_CHUNK
    nc = seq // q
    m = batch * seq
    wp = SSD_WIDTH_PAD

    def tok(b, c):
        return b * nc + c

    def const_spec(arr):
        return pl.BlockSpec(arr.shape, lambda b, c: (0, 0))

    return pl.pallas_call(
        _ssd_kernel,
        out_shape=jax.ShapeDtypeStruct((m, out_width), BF16),
        grid=(batch, nc),
        in_specs=[pl.BlockSpec((q, wp), lambda b, c: (tok(b, c), 0)),
                  pl.BlockSpec((q, wp), lambda b, c: (tok(b, c), 1)),
                  pl.BlockSpec((q, SSD_BC_WIDTH), lambda b, c: (tok(b, c), 0)),
                  pl.BlockSpec((q, LANES), lambda b, c: (tok(b, c), 0)),
                  const_spec(cwx), const_spec(cbx), const_spec(cwbc), const_spec(cbbc),
                  const_spec(dtb), const_spec(alog), const_spec(dskip), const_spec(nw),
                  const_spec(head_to_feat), const_spec(head_to_tile)],
        out_specs=pl.BlockSpec((q, SSD_WIDTH), lambda b, c: (tok(b, c), 0)),
        scratch_shapes=[pltpu.VMEM((q + CONV_HALO_ROWS, wp), F32),
                        pltpu.VMEM((q + CONV_HALO_ROWS, SSD_BC_WIDTH), F32),
                        pltpu.VMEM((SSD_STATE, wp), F32)],
        compiler_params=_cparams(("parallel", "arbitrary")),
        name="ssd_chunk_scan",
    )(zx, zx, bc, dt, cwx, cbx, cwbc, cbbc, dtb, alog, dskip, nw, head_to_feat, head_to_tile)


def _alibi_slopes(n):
    def pow2(m):
        start = 2.0 ** (-8.0 / m)
        return [start ** (i + 1) for i in range(m)]
    if math.log2(n).is_integer():
        s = pow2(n)
    else:
        c = 2 ** int(math.floor(math.log2(n)))
        s = pow2(c) + pow2(2 * c)[0::2][: n - c]
    return [float(np.float32(v)) for v in s]


def _alibi_base(n_query, n_key, key_offset):
    qi = lax.broadcasted_iota(jnp.int32, (n_query, n_key), 0)
    kj = lax.broadcasted_iota(jnp.int32, (n_query, n_key), 1)
    dist = qi - kj + key_offset
    return jnp.where((dist >= 0) & (dist <= ATT_BLOCK), -dist.astype(F32), -jnp.inf)


def _attn_kernel(q_ref, k_ref, v_ref, slope_ref, mixed_ref, o_ref, o_acc, lse_acc):
    seq = q_ref.shape[0]
    c = ATT_BLOCK

    def attend(g, d, start, qb, bases):
        stride = None if d == 1 else d
        rows = pl.ds(start, qb, stride=stride)
        q = q_ref[rows, :].astype(BF16)
        tok_slope = slope_ref[0][:, :qb] * float(d)
        s_cur = _dot_nt(q, k_ref[rows, :].astype(BF16)) + tok_slope * bases[0]
        mx = jnp.max(s_cur, axis=-1, keepdims=True)
        if bases[1] is not None:
            prows = pl.ds(start - c * d, c, stride=stride)
            s_prev = _dot_nt(q, k_ref[prows, :].astype(BF16)) + tok_slope[:, :c] * bases[1]
            mx = jnp.maximum(mx, jnp.max(s_prev, axis=-1, keepdims=True))
        p_cur = jnp.exp(s_cur - mx)
        den = jnp.sum(p_cur, axis=-1, keepdims=True)
        o = _dot(p_cur.astype(BF16), v_ref[rows, :].astype(BF16))
        if bases[1] is not None:
            p_prev = jnp.exp(s_prev - mx)
            den = den + jnp.sum(p_prev, axis=-1, keepdims=True)
            o = o + _dot(p_prev.astype(BF16), v_ref[prows, :].astype(BF16))
        o_acc[g, rows, :] = o / den
        lse_acc[g, rows, :] = jnp.broadcast_to(mx + jnp.log(den), (qb, LANES))

    for g, (_, d) in enumerate(ATT_BRANCHES):
        sub_len = seq // d
        qb = min(ATT_QUERY_BLOCK, sub_len)
        n_blocks = sub_len // qb
        base_cur = _alibi_base(qb, qb, 0)
        base_prev = _alibi_base(qb, c, c) if n_blocks > 1 else None

        def residue(r, carry, g=g, d=d, qb=qb, n_blocks=n_blocks, base_cur=base_cur, base_prev=base_prev):
            attend(g, d, r, qb, (base_cur, None))

            def later(nb, inner):
                offset = nb * (qb * d)
                attend(g, d, r + (pl.multiple_of(offset, qb) if d == 1 else offset), qb, (base_cur, base_prev))
                return inner

            if n_blocks > 1:
                lax.fori_loop(1, n_blocks, later, 0)
            return carry

        if d == 1:
            residue(0, 0)
        else:
            lax.fori_loop(0, d, residue, 0, unroll=2 if n_blocks == 1 else 1)

    def merge(i, carry):
        rows = pl.ds(pl.multiple_of(i * ATT_MERGE_ROWS, ATT_MERGE_ROWS), ATT_MERGE_ROWS)
        lses = [lse_acc[g, rows, :] for g in range(len(ATT_BRANCHES))]
        mx = functools.reduce(jnp.maximum, lses)
        es = [jnp.exp(l - mx) for l in lses]
        num = sum(e * o_acc[g, rows, :] for g, e in enumerate(es))
        o_ref[rows, :] = (num / sum(es)).astype(o_ref.dtype)
        return carry

    lax.fori_loop(0, seq // ATT_MERGE_ROWS, merge, 0)


def _dilated_attention(qkv, mixed, *, batch, seq):
    e = ATT_HEAD_DIM
    n_branch = len(ATT_BRANCHES)
    assert all(seq % (d * ATT_BLOCK) == 0 for _, d in ATT_BRANCHES)
    first_col_block = (mixed.shape[1] - ATT_WIDTH) // e
    assert first_col_block * e + ATT_WIDTH == mixed.shape[1]
    slopes = np.repeat(np.asarray(_alibi_slopes(ATT_HEADS), np.float32)[:, None, None], ATT_QUERY_BLOCK, axis=2)

    def head_spec(part):
        return pl.BlockSpec((seq, e), lambda b, h: (b, part * ATT_HEADS + h))

    return pl.pallas_call(
        _attn_kernel,
        out_shape=jax.ShapeDtypeStruct(mixed.shape, mixed.dtype),
        grid=(batch, ATT_HEADS),
        in_specs=[head_spec(0), head_spec(1), head_spec(2),
                  pl.BlockSpec((1, 1, ATT_QUERY_BLOCK), lambda b, h: (h, 0, 0)),
                  pl.BlockSpec(memory_space=pl.ANY)],
        out_specs=pl.BlockSpec((seq, e), lambda b, h: (b, first_col_block + h)),
        input_output_aliases={4: 0},
        scratch_shapes=[pltpu.VMEM((n_branch, seq, e), F32),
                        pltpu.VMEM((n_branch, seq, LANES), F32)],
        compiler_params=_cparams(("parallel", "parallel")),
        name="dilated_attention",
    )(qkv, qkv, qkv, jnp.asarray(slopes), mixed)


def _pad_groups(t):
    lead = t.shape[:-1]
    t = t.reshape(*lead, SSD_GROUPS, SSD_GROUP_WIDTH)
    t = jnp.pad(t, [(0, 0)] * len(lead) + [(0, 0), (0, SSD_GROUP_PAD - SSD_GROUP_WIDTH)])
    return t.reshape(*lead, SSD_WIDTH_PAD)


def _pad_last(t, n):
    return jnp.pad(t, [(0, 0)] * (t.ndim - 1) + [(0, n - t.shape[-1])])


def _head_selectors():
    head_to_feat = np.zeros((LANES, SSD_WIDTH_PAD), np.float32)
    head_to_tile = np.zeros((LANES, SSD_HEADS * SSD_CHUNK), np.float32)
    for j in range(SSD_HEADS):
        g, jj = divmod(j, SSD_HEADS_PER_GROUP)
        f0 = g * SSD_GROUP_PAD + jj * SSD_HEAD_DIM
        head_to_feat[j, f0:f0 + SSD_HEAD_DIM] = 1.0
        head_to_tile[j, j * SSD_CHUNK:(j + 1) * SSD_CHUNK] = 1.0
    return jnp.asarray(head_to_feat, BF16), jnp.asarray(head_to_tile, BF16)


def _ffn(h, xn, post_w, w_gate, w_up, w_down, next_pre_w):
    act = _gateup(xn, w_gate, w_up)
    f = _matmul(act, w_down.astype(BF16), F32, k_steps=2, name="ffn_down")
    return _postnorm(f, h, post_w, 0.5, next_pre_w)


def kernel(x, p, ffn1_pre_w, ffn1_post_w, ffn1_w_gate, ffn1_w_up, ffn1_w_down, mix_pre_w, mix_post_w, w_in, conv_w, conv_b, dt_bias, a_log, d_skip, ssd_norm_w, w_out, ffn2_pre_w, ffn2_post_w, ffn2_w_gate, ffn2_w_up, ffn2_w_down, ple_pre_w, ple_post_w, w_ple_gate, w_ple_proj):
    batch, seq, d_model = x.shape
    depth = w_in.shape[0]
    m = batch * seq
    h = x.reshape(m, d_model)
    head_to_feat, head_to_tile = _head_selectors()
    i1 = SSD_WIDTH
    i2 = i1 + SSD_WIDTH
    i3 = i2 + SSD_BC_WIDTH
    i4 = i3 + SSD_HEADS

    for i in range(depth):
        xn = _rmsnorm(h, ffn1_pre_w[i])
        h, u = _ffn(h, xn, ffn1_post_w[i], ffn1_w_gate[i], ffn1_w_up[i], ffn1_w_down[i], mix_pre_w[i])

        wi = w_in[i]
        zx = _matmul_pad_groups(u, wi[:, :i2].astype(BF16), name="in_proj_zx")
        bc = _matmul(u, wi, BF16, w_cols=(i2, SSD_BC_WIDTH), name="in_proj_bc")
        dt = _matmul(u, wi, F32, w_cols=(i3, LANES), name="in_proj_dt")
        i5 = i4 + ATT_WIDTH
        w_qkv = jnp.concatenate([wi[:, i4:i5] * (ATT_HEAD_DIM ** -0.5), wi[:, i5:]], axis=1).astype(BF16)
        qkv = _matmul(u, w_qkv, F32, name="in_proj_qkv")

        cw, cb = conv_w[i], conv_b[i]
        ssd_consts = (
            _pad_groups(cw[:, :SSD_WIDTH]), _pad_groups(cb[None, :SSD_WIDTH]),
            cw[:, SSD_WIDTH:], cb[None, SSD_WIDTH:],
            _pad_last(dt_bias[i][None, :], LANES),
            _pad_last(a_log[i][None, :], LANES),
            _pad_groups(jnp.repeat(d_skip[i], SSD_HEAD_DIM)[None, :]),
            _pad_groups(ssd_norm_w[i][None, :]),
            head_to_feat, head_to_tile)
        mixed = _ssd(zx, bc, dt, ssd_consts, batch=batch, seq=seq, out_width=SSD_WIDTH + ATT_WIDTH)
        mixed = _dilated_attention(qkv, mixed, batch=batch, seq=seq)
        mix = _matmul(mixed, w_out[i].astype(BF16), F32, name="out_proj")
        h, xn = _postnorm(mix, h, mix_post_w[i], 1.0, ffn2_pre_w[i])

        h, xn = _ffn(h, xn, ffn2_post_w[i], ffn2_w_gate[i], ffn2_w_up[i], ffn2_w_down[i], ple_pre_w[i])

        gate = _matmul(xn, w_ple_gate[i].astype(BF16), F32, name="ple_gate")
        h = _ple(gate, p[i].reshape(m, -1), w_ple_proj[i].astype(BF16), h, ple_post_w[i])

    return h.reshape(batch, seq, d_model)
```

```python
import functools
import math

import numpy as np
import jax
import jax.numpy as jnp
from jax import lax
from jax.experimental import pallas as pl
from jax.experimental.pallas import tpu as pltpu

F32 = jnp.float32
BF16 = jnp.bfloat16

RMS_EPS = 1e-6

LANES = 128
SUBLANES = 8
VMEM_LIMIT_BYTES = 58 * 1024 * 1024

SSD_HEAD_DIM = 64
SSD_HEADS = 40
SSD_GROUPS = 8
SSD_HEADS_PER_GROUP = SSD_HEADS // SSD_GROUPS
SSD_GROUP_WIDTH = SSD_HEADS_PER_GROUP * SSD_HEAD_DIM
SSD_GROUP_PAD = 384
SSD_WIDTH = SSD_HEADS * SSD_HEAD_DIM
SSD_WIDTH_PAD = SSD_GROUPS * SSD_GROUP_PAD
SSD_STATE = 128
SSD_CONV = 4
SSD_CHUNK = 128
SSD_BC_WIDTH = 2 * SSD_GROUPS * SSD_STATE
CONV_HALO_ROWS = SUBLANES

ATT_HEAD_DIM = 128
ATT_HEADS = 12
ATT_WIDTH = ATT_HEADS * ATT_HEAD_DIM
ATT_BRANCHES = ((128, 1), (512, 4), (2048, 16))
ATT_BLOCK = 128
ATT_QUERY_BLOCK = 512
ATT_MERGE_ROWS = 256


def _cparams(semantics):
    return pltpu.CompilerParams(dimension_semantics=semantics, vmem_limit_bytes=VMEM_LIMIT_BYTES)


def _dot(a, b):
    return jnp.dot(a, b, preferred_element_type=F32)


def _dot_nt(a, b):
    return lax.dot_general(a, b, (((1,), (1,)), ((), ())), preferred_element_type=F32)


def _dot_tn(a, b):
    return lax.dot_general(a, b, (((0,), (0,)), ((), ())), preferred_element_type=F32)


def _split_bf16(x):
    hi = x.astype(BF16)
    lo = (x - hi.astype(F32)).astype(BF16)
    return hi, lo


def _dot_split_lhs(x, sel):
    hi, lo = _split_bf16(x)
    return _dot(hi, sel) + _dot(lo, sel)


def _silu(x):
    hx = 0.5 * x
    return hx + hx * jnp.tanh(hx)


def _rms_scale(v):
    return lax.rsqrt(jnp.mean(v * v, axis=-1, keepdims=True) + RMS_EPS)


def _row_slabs(n_rows, rows_per_step, body, unroll=1):
    def step(r, carry):
        body(pl.ds(pl.multiple_of(r * rows_per_step, rows_per_step), rows_per_step))
        return carry

    lax.fori_loop(0, n_rows // rows_per_step, step, 0, unroll=unroll)


def _rmsnorm_kernel(x_ref, w_ref, o_ref):
    def body(rows):
        x = x_ref[rows, :]
        o_ref[rows, :] = (x * _rms_scale(x) * w_ref[...]).astype(o_ref.dtype)

    _row_slabs(x_ref.shape[0], 16, body, unroll=4)


def _rmsnorm(x, w, *, tm=256):
    m, d = x.shape
    return pl.pallas_call(
        _rmsnorm_kernel,
        out_shape=jax.ShapeDtypeStruct((m, d), BF16),
        grid=(m // tm,),
        in_specs=[pl.BlockSpec((tm, d), lambda i: (i, 0)),
                  pl.BlockSpec((1, d), lambda i: (0, 0))],
        out_specs=pl.BlockSpec((tm, d), lambda i: (i, 0)),
        compiler_params=_cparams(("parallel",)),
        name="rmsnorm",
    )(x, w.reshape(1, d))


def _matmul_kernel(a_ref, w_ref, o_ref, *, k_steps):
    part = _dot(a_ref[...], w_ref[...].astype(BF16))
    if k_steps == 1:
        o_ref[...] = part.astype(o_ref.dtype)
    else:
        k = pl.program_id(2)

        @pl.when(k == 0)
        def _():
            o_ref[...] = part

        @pl.when(k > 0)
        def _():
            o_ref[...] += part


def _matmul(a, w, out_dtype, *, tm=1024, tn=512, k_steps=1, w_cols=None, name="matmul"):
    m, k = a.shape
    start, n = (0, w.shape[1]) if w_cols is None else w_cols
    tn = min(tn, n)
    tk = k // k_steps
    assert k_steps == 1 or out_dtype == F32
    assert start % tn == 0 and n % tn == 0
    first = start // tn
    return pl.pallas_call(
        functools.partial(_matmul_kernel, k_steps=k_steps),
        out_shape=jax.ShapeDtypeStruct((m, n), out_dtype),
        grid=(m // tm, n // tn, k_steps),
        in_specs=[pl.BlockSpec((tm, tk), lambda i, j, kk: (i, kk)),
                  pl.BlockSpec((tk, tn), lambda i, j, kk: (kk, first + j))],
        out_specs=pl.BlockSpec((tm, tn), lambda i, j, kk: (i, j)),
        compiler_params=_cparams(("parallel", "arbitrary", "arbitrary")),
        name=name,
    )(a, w)


def _matmul_pad_groups_kernel(a_ref, w_ref, o_ref, *, groups_per_tile):
    res = _dot(a_ref[...], w_ref[...].astype(BF16))
    pad = jnp.zeros((res.shape[0], SSD_GROUP_PAD - SSD_GROUP_WIDTH), F32)
    parts = []
    for g in range(groups_per_tile):
        parts += [res[:, g * SSD_GROUP_WIDTH:(g + 1) * SSD_GROUP_WIDTH], pad]
    o_ref[...] = jnp.concatenate(parts, axis=1).astype(o_ref.dtype)


def _matmul_pad_groups(a, w, *, tm=1024, groups_per_tile=4, name="matmul_pad_groups"):
    m, k = a.shape
    n = w.shape[1]
    tn_in = groups_per_tile * SSD_GROUP_WIDTH
    tn_out = groups_per_tile * SSD_GROUP_PAD
    assert n % tn_in == 0 and tn_in % LANES == 0
    return pl.pallas_call(
        functools.partial(_matmul_pad_groups_kernel, groups_per_tile=groups_per_tile),
        out_shape=jax.ShapeDtypeStruct((m, n // tn_in * tn_out), BF16),
        grid=(m // tm, n // tn_in),
        in_specs=[pl.BlockSpec((tm, k), lambda i, j: (i, 0)),
                  pl.BlockSpec((k, tn_in), lambda i, j: (0, j))],
        out_specs=pl.BlockSpec((tm, tn_out), lambda i, j: (i, j)),
        compiler_params=_cparams(("parallel", "arbitrary")),
        name=name,
    )(a, w)


def _split_w_in_kernel(w_ref, zx_ref, qkv_ref, *, zx_cols, q_start, q_scale):
    zx_ref[...] = w_ref[:, :zx_cols].astype(zx_ref.dtype)
    q_end = q_start + ATT_WIDTH
    qkv_ref[:, :ATT_WIDTH] = (w_ref[:, q_start:q_end] * q_scale).astype(qkv_ref.dtype)
    qkv_ref[:, ATT_WIDTH:] = w_ref[:, q_end:].astype(qkv_ref.dtype)


def _split_w_in(w, zx_cols, q_start, q_scale, *, tr=256):
    k, n = w.shape
    qkv_cols = n - q_start
    return pl.pallas_call(
        functools.partial(_split_w_in_kernel, zx_cols=zx_cols, q_start=q_start, q_scale=q_scale),
        out_shape=(jax.ShapeDtypeStruct((k, zx_cols), BF16), jax.ShapeDtypeStruct((k, qkv_cols), BF16)),
        grid=(k // tr,),
        in_specs=[pl.BlockSpec((tr, n), lambda r: (r, 0))],
        out_specs=(pl.BlockSpec((tr, zx_cols), lambda r: (r, 0)),
                   pl.BlockSpec((tr, qkv_cols), lambda r: (r, 0))),
        compiler_params=_cparams(("parallel",)),
        name="split_w_in",
    )(w)


def _gateup_kernel(a_ref, wg_ref, wu_ref, o_ref):
    a = a_ref[...]
    g = _dot(a, wg_ref[...].astype(BF16))
    u = _dot(a, wu_ref[...].astype(BF16))
    o_ref[...] = (_silu(g) * u).astype(o_ref.dtype)


def _gateup(a, wg, wu, *, tm=2048, tn=256):
    m, k = a.shape
    n = wg.shape[1]
    return pl.pallas_call(
        _gateup_kernel,
        out_shape=jax.ShapeDtypeStruct((m, n), BF16),
        grid=(m // tm, n // tn),
        in_specs=[pl.BlockSpec((tm, k), lambda i, j: (i, 0), pipeline_mode=pl.Buffered(1)),
                  pl.BlockSpec((k, tn), lambda i, j: (0, j)),
                  pl.BlockSpec((k, tn), lambda i, j: (0, j))],
        out_specs=pl.BlockSpec((tm, tn), lambda i, j: (i, j)),
        compiler_params=_cparams(("parallel", "arbitrary")),
        name="swiglu_gateup",
    )(a, wg, wu)


def _postnorm_kernel(f_ref, h_ref, nw_ref, next_w_ref, o_ref, xn_ref, *, scale):
    def body(rows):
        f = f_ref[rows, :]
        hn = h_ref[rows, :] + scale * (f * _rms_scale(f) * nw_ref[...])
        o_ref[rows, :] = hn
        xn_ref[rows, :] = (hn * _rms_scale(hn) * next_w_ref[...]).astype(xn_ref.dtype)

    _row_slabs(f_ref.shape[0], 16, body, unroll=4)


def _postnorm(f, h, nw, scale, next_w, *, tm=256):
    m, n = f.shape
    row_spec = pl.BlockSpec((tm, n), lambda i: (i, 0))
    vec_spec = pl.BlockSpec((1, n), lambda i: (0, 0))
    return pl.pallas_call(
        functools.partial(_postnorm_kernel, scale=scale),
        out_shape=(jax.ShapeDtypeStruct((m, n), F32), jax.ShapeDtypeStruct((m, n), BF16)),
        grid=(m // tm,),
        in_specs=[row_spec, row_spec, vec_spec, vec_spec],
        out_specs=(row_spec, row_spec),
        compiler_params=_cparams(("parallel",)),
        name="postnorm_residual",
    )(f, h, nw.reshape(1, n), next_w.reshape(1, n))


def _ple_kernel(g_ref, p_ref, wp_ref, h_ref, nw_ref, o_ref):
    def body(rows):
        proj = _dot(p_ref[rows, :].astype(BF16), wp_ref[...])
        f = jax.nn.sigmoid(g_ref[rows, :]) * proj
        o_ref[rows, :] = h_ref[rows, :] + f * _rms_scale(f) * nw_ref[...]

    _row_slabs(g_ref.shape[0], 64, body)


def _ple(g, p, wp, h, nw, *, tm=256):
    m, n = g.shape
    pdim = p.shape[1]
    row_spec = pl.BlockSpec((tm, n), lambda i: (i, 0))
    return pl.pallas_call(
        _ple_kernel,
        out_shape=jax.ShapeDtypeStruct((m, n), F32),
        grid=(m // tm,),
        in_specs=[row_spec,
                  pl.BlockSpec((tm, pdim), lambda i: (i, 0)),
                  pl.BlockSpec((pdim, n), lambda i: (0, 0)),
                  row_spec,
                  pl.BlockSpec((1, n), lambda i: (0, 0))],
        out_specs=row_spec,
        compiler_params=_cparams(("parallel",)),
        name="ple_gate_postnorm",
    )(g, p, wp, h, nw.reshape(1, n))


def _ssd_kernel(z_ref, x_ref, bc_ref, dt_ref, cwx_ref, cbx_ref, cwbc_ref, cbbc_ref, dtb_ref, alog_ref,
                dskip_ref, nw_ref, head_to_feat_ref, head_to_tile_ref, o_ref, xbuf, bcbuf, state):
    c = pl.program_id(1)
    q = SSD_CHUNK
    halo = CONV_HALO_ROWS

    @pl.when(c == 0)
    def _():
        xbuf[0:halo, :] = jnp.zeros((halo, xbuf.shape[1]), F32)
        bcbuf[0:halo, :] = jnp.zeros((halo, bcbuf.shape[1]), F32)
        state[...] = jnp.zeros(state.shape, F32)

    xbuf[halo:halo + q, :] = x_ref[...].astype(F32)
    bcbuf[halo:halo + q, :] = bc_ref[...].astype(F32)

    v = dt_ref[...] + dtb_ref[...]
    dt = jnp.maximum(v, 0.0) + jnp.log(1.0 + jnp.exp(-jnp.abs(v)))
    head_lane = lax.broadcasted_iota(jnp.int32, dt.shape, 1)
    dt = jnp.where(head_lane < SSD_HEADS, dt, 0.0)
    da = dt * -jnp.exp(alog_ref[...])
    row = lax.broadcasted_iota(jnp.int32, (q, q), 0)
    col = lax.broadcasted_iota(jnp.int32, (q, q), 1)
    causal = row >= col
    tril = jnp.where(causal, 1.0, 0.0).astype(BF16)
    da_hi, da_lo = _split_bf16(da)
    a_cs = _dot(tril, da_hi) + _dot(tril, da_lo)
    a_cs_t = a_cs.T
    head_to_feat = head_to_feat_ref[...]
    a_feat = _dot_split_lhs(a_cs, head_to_feat)
    dt_feat = _dot_split_lhs(dt, head_to_feat)
    a_tile = _dot_split_lhs(a_cs, head_to_tile_ref[...])

    lane = lax.broadcasted_iota(jnp.int32, (q, LANES), 1)
    heads_per_tile = LANES // SSD_HEAD_DIM

    for g in range(SSD_GROUPS):
        lo, hi = g * SSD_GROUP_PAD, (g + 1) * SSD_GROUP_PAD
        blo, clo = g * SSD_STATE, (SSD_GROUPS + g) * SSD_STATE

        def conv(buf, w_ref, b_ref, c0, c1):
            acc = b_ref[:, c0:c1] + w_ref[0:1, c0:c1] * buf[pl.ds(halo - 3, q), c0:c1]
            for t in range(1, SSD_CONV):
                acc = acc + w_ref[t:t + 1, c0:c1] * buf[pl.ds(halo - 3 + t, q), c0:c1]
            return _silu(acc)

        xg = conv(xbuf, cwx_ref, cbx_ref, lo, hi)
        bg = conv(bcbuf, cwbc_ref, cbbc_ref, blo, blo + SSD_STATE)
        cg = conv(bcbuf, cwbc_ref, cbbc_ref, clo, clo + SSD_STATE)
        bg16 = bg.astype(BF16)
        cg16 = cg.astype(BF16)

        a_g = a_feat[:, lo:hi]
        a_last = a_g[q - 1:q, :]
        xdt = xg * dt_feat[:, lo:hi]

        cb = _dot_nt(cg16, bg16)
        m_parts = []
        for jj in range(SSD_HEADS_PER_GROUP):
            j = g * SSD_HEADS_PER_GROUP + jj
            seg = a_tile[:, j * q:(j + 1) * q] - a_cs_t[j:j + 1, :]
            decay = jnp.exp(jnp.where(causal, seg, -jnp.inf))
            m_parts.append((cb * decay).astype(BF16))
        y_tiles = []
        for t in range(SSD_GROUP_PAD // LANES):
            xt = xdt[:, t * LANES:(t + 1) * LANES]
            heads = [jj for jj in range(t * heads_per_tile, (t + 1) * heads_per_tile) if jj < SSD_HEADS_PER_GROUP]
            if len(heads) == 1:
                y_tiles.append(_dot(m_parts[heads[0]], xt.astype(BF16)))
            else:
                xs = [jnp.where(lane // SSD_HEAD_DIM == jj % heads_per_tile, xt, 0.0).astype(BF16) for jj in heads]
                y_tiles.append(_dot(jnp.concatenate([m_parts[jj] for jj in heads], axis=1),
                                    jnp.concatenate(xs, axis=0)))
        y = jnp.concatenate(y_tiles, axis=1)

        prev = state[:, lo:hi]
        y = y + _dot(cg16, prev.astype(BF16)) * jnp.exp(a_g)
        xdd = (xdt * jnp.exp(a_last - a_g)).astype(BF16)
        state[:, lo:hi] = prev * jnp.exp(a_last) + _dot_tn(bg16, xdd)

        y = y + xg * dskip_ref[:, lo:hi]
        y = y * _silu(z_ref[:, lo:hi].astype(F32))
        ms = jnp.sum(y * y, axis=-1, keepdims=True) * (1.0 / SSD_GROUP_WIDTH)
        y = y * lax.rsqrt(ms + RMS_EPS) * nw_ref[:, lo:hi]
        o_ref[:, g * SSD_GROUP_WIDTH:(g + 1) * SSD_GROUP_WIDTH] = y[:, :SSD_GROUP_WIDTH].astype(o_ref.dtype)

    xbuf[0:halo, :] = xbuf[q:q + halo, :]
    bcbuf[0:halo, :] = bcbuf[q:q + halo, :]


def _ssd(zx, bc, dt, consts, *, batch, seq, out_width):
    (cwx, cbx, cwbc, cbbc, dtb, alog, dskip, nw, head_to_feat, head_to_tile) = consts
    q = SSD_CHUNK
    nc = seq // q
    m = batch * seq
    wp = SSD_WIDTH_PAD

    def tok(b, c):
        return b * nc + c

    def const_spec(arr):
        return pl.BlockSpec(arr.shape, lambda b, c: (0, 0))

    return pl.pallas_call(
        _ssd_kernel,
        out_shape=jax.ShapeDtypeStruct((m, out_width), BF16),
        grid=(batch, nc),
        in_specs=[pl.BlockSpec((q, wp), lambda b, c: (tok(b, c), 0)),
                  pl.BlockSpec((q, wp), lambda b, c: (tok(b, c), 1)),
                  pl.BlockSpec((q, SSD_BC_WIDTH), lambda b, c: (tok(b, c), 0)),
                  pl.BlockSpec((q, LANES), lambda b, c: (tok(b, c), 0)),
                  const_spec(cwx), const_spec(cbx), const_spec(cwbc), const_spec(cbbc),
                  const_spec(dtb), const_spec(alog), const_spec(dskip), const_spec(nw),
                  const_spec(head_to_feat), const_spec(head_to_tile)],
        out_specs=pl.BlockSpec((q, SSD_WIDTH), lambda b, c: (tok(b, c), 0)),
        scratch_shapes=[pltpu.VMEM((q + CONV_HALO_ROWS, wp), F32),
                        pltpu.VMEM((q + CONV_HALO_ROWS, SSD_BC_WIDTH), F32),
                        pltpu.VMEM((SSD_STATE, wp), F32)],
        compiler_params=_cparams(("parallel", "arbitrary")),
        name="ssd_chunk_scan",
    )(zx, zx, bc, dt, cwx, cbx, cwbc, cbbc, dtb, alog, dskip, nw, head_to_feat, head_to_tile)


def _alibi_slopes(n):
    def pow2(m):
        start = 2.0 ** (-8.0 / m)
        return [start ** (i + 1) for i in range(m)]
    if math.log2(n).is_integer():
        s = pow2(n)
    else:
        c = 2 ** int(math.floor(math.log2(n)))
        s = pow2(c) + pow2(2 * c)[0::2][: n - c]
    return [float(np.float32(v)) for v in s]


def _alibi_base(n_query, n_key, key_offset):
    qi = lax.broadcasted_iota(jnp.int32, (n_query, n_key), 0)
    kj = lax.broadcasted_iota(jnp.int32, (n_query, n_key), 1)
    dist = qi - kj + key_offset
    return jnp.where((dist >= 0) & (dist <= ATT_BLOCK), -dist.astype(F32), -jnp.inf)


def _attn_kernel(q_ref, k_ref, v_ref, slope_ref, mixed_ref, o_ref, o_acc, lse_acc):
    seq = q_ref.shape[0]
    c = ATT_BLOCK

    def attend(g, d, start, qb, bases):
        stride = None if d == 1 else d
        rows = pl.ds(start, qb, stride=stride)
        q = q_ref[rows, :].astype(BF16)
        tok_slope = slope_ref[0][:, :qb] * float(d)
        s_cur = _dot_nt(q, k_ref[rows, :].astype(BF16)) + tok_slope * bases[0]
        mx = jnp.max(s_cur, axis=-1, keepdims=True)
        if bases[1] is not None:
            prows = pl.ds(start - c * d, c, stride=stride)
            s_prev = _dot_nt(q, k_ref[prows, :].astype(BF16)) + tok_slope[:, :c] * bases[1]
            mx = jnp.maximum(mx, jnp.max(s_prev, axis=-1, keepdims=True))
        p_cur = jnp.exp(s_cur - mx)
        den = jnp.sum(p_cur, axis=-1, keepdims=True)
        o = _dot(p_cur.astype(BF16), v_ref[rows, :].astype(BF16))
        if bases[1] is not None:
            p_prev = jnp.exp(s_prev - mx)
            den = den + jnp.sum(p_prev, axis=-1, keepdims=True)
            o = o + _dot(p_prev.astype(BF16), v_ref[prows, :].astype(BF16))
        o_acc[g, rows, :] = o / den
        lse_acc[g, rows, :] = jnp.broadcast_to(mx + jnp.log(den), (qb, LANES))

    for g, (_, d) in enumerate(ATT_BRANCHES):
        sub_len = seq // d
        qb = min(ATT_QUERY_BLOCK, sub_len)
        n_blocks = sub_len // qb
        base_cur = _alibi_base(qb, qb, 0)
        base_prev = _alibi_base(qb, c, c) if n_blocks > 1 else None

        def residue(r, carry, g=g, d=d, qb=qb, n_blocks=n_blocks, base_cur=base_cur, base_prev=base_prev):
            attend(g, d, r, qb, (base_cur, None))

            def later(nb, inner):
                offset = nb * (qb * d)
                attend(g, d, r + (pl.multiple_of(offset, qb) if d == 1 else offset), qb, (base_cur, base_prev))
                return inner

            if n_blocks > 1:
                lax.fori_loop(1, n_blocks, later, 0)
            return carry

        if d == 1:
            residue(0, 0)
        else:
            lax.fori_loop(0, d, residue, 0, unroll=2 if n_blocks == 1 else 1)

    def merge(i, carry):
        rows = pl.ds(pl.multiple_of(i * ATT_MERGE_ROWS, ATT_MERGE_ROWS), ATT_MERGE_ROWS)
        lses = [lse_acc[g, rows, :] for g in range(len(ATT_BRANCHES))]
        mx = functools.reduce(jnp.maximum, lses)
        es = [jnp.exp(l - mx) for l in lses]
        num = sum(e * o_acc[g, rows, :] for g, e in enumerate(es))
        o_ref[rows, :] = (num / sum(es)).astype(o_ref.dtype)
        return carry

    lax.fori_loop(0, seq // ATT_MERGE_ROWS, merge, 0)


def _dilated_attention(qkv, mixed, *, batch, seq):
    e = ATT_HEAD_DIM
    n_branch = len(ATT_BRANCHES)
    assert all(seq % (d * ATT_BLOCK) == 0 for _, d in ATT_BRANCHES)
    first_col_block = (mixed.shape[1] - ATT_WIDTH) // e
    assert first_col_block * e + ATT_WIDTH == mixed.shape[1]
    slopes = np.repeat(np.asarray(_alibi_slopes(ATT_HEADS), np.float32)[:, None, None], ATT_QUERY_BLOCK, axis=2)

    def head_spec(part):
        return pl.BlockSpec((seq, e), lambda b, h: (b, part * ATT_HEADS + h))

    return pl.pallas_call(
        _attn_kernel,
        out_shape=jax.ShapeDtypeStruct(mixed.shape, mixed.dtype),
        grid=(batch, ATT_HEADS),
        in_specs=[head_spec(0), head_spec(1), head_spec(2),
                  pl.BlockSpec((1, 1, ATT_QUERY_BLOCK), lambda b, h: (h, 0, 0)),
                  pl.BlockSpec(memory_space=pl.ANY)],
        out_specs=pl.BlockSpec((seq, e), lambda b, h: (b, first_col_block + h)),
        input_output_aliases={4: 0},
        scratch_shapes=[pltpu.VMEM((n_branch, seq, e), F32),
                        pltpu.VMEM((n_branch, seq, LANES), F32)],
        compiler_params=_cparams(("parallel", "parallel")),
        name="dilated_attention",
    )(qkv, qkv, qkv, jnp.asarray(slopes), mixed)


def _pad_groups(t):
    lead = t.shape[:-1]
    t = t.reshape(*lead, SSD_GROUPS, SSD_GROUP_WIDTH)
    t = jnp.pad(t, [(0, 0)] * len(lead) + [(0, 0), (0, SSD_GROUP_PAD - SSD_GROUP_WIDTH)])
    return t.reshape(*lead, SSD_WIDTH_PAD)


def _pad_last(t, n):
    return jnp.pad(t, [(0, 0)] * (t.ndim - 1) + [(0, n - t.shape[-1])])


def _head_selectors():
    head_to_feat = np.zeros((LANES, SSD_WIDTH_PAD), np.float32)
    head_to_tile = np.zeros((LANES, SSD_HEADS * SSD_CHUNK), np.float32)
    for j in range(SSD_HEADS):
        g, jj = divmod(j, SSD_HEADS_PER_GROUP)
        f0 = g * SSD_GROUP_PAD + jj * SSD_HEAD_DIM
        head_to_feat[j, f0:f0 + SSD_HEAD_DIM] = 1.0
        head_to_tile[j, j * SSD_CHUNK:(j + 1) * SSD_CHUNK] = 1.0
    return jnp.asarray(head_to_feat, BF16), jnp.asarray(head_to_tile, BF16)


def _ffn(h, xn, post_w, w_gate, w_up, w_down, next_pre_w):
    act = _gateup(xn, w_gate, w_up)
    f = _matmul(act, w_down.astype(BF16), F32, k_steps=2, name="ffn_down")
    return _postnorm(f, h, post_w, 0.5, next_pre_w)


def kernel(x, p, ffn1_pre_w, ffn1_post_w, ffn1_w_gate, ffn1_w_up, ffn1_w_down, mix_pre_w, mix_post_w, w_in, conv_w, conv_b, dt_bias, a_log, d_skip, ssd_norm_w, w_out, ffn2_pre_w, ffn2_post_w, ffn2_w_gate, ffn2_w_up, ffn2_w_down, ple_pre_w, ple_post_w, w_ple_gate, w_ple_proj):
    batch, seq, d_model = x.shape
    depth = w_in.shape[0]
    m = batch * seq
    h = x.reshape(m, d_model)
    head_to_feat, head_to_tile = _head_selectors()
    i1 = SSD_WIDTH
    i2 = i1 + SSD_WIDTH
    i3 = i2 + SSD_BC_WIDTH
    i4 = i3 + SSD_HEADS

    for i in range(depth):
        xn = _rmsnorm(h, ffn1_pre_w[i])
        h, u = _ffn(h, xn, ffn1_post_w[i], ffn1_w_gate[i], ffn1_w_up[i], ffn1_w_down[i], mix_pre_w[i])

        wi = w_in[i]
        w_zx, w_qkv = _split_w_in(wi, i2, i4, ATT_HEAD_DIM ** -0.5)
        zx = _matmul_pad_groups(u, w_zx, name="in_proj_zx")
        bc = _matmul(u, wi, BF16, w_cols=(i2, SSD_BC_WIDTH), name="in_proj_bc")
        dt = _matmul(u, wi, F32, w_cols=(i3, LANES), name="in_proj_dt")
        qkv = _matmul(u, w_qkv, F32, name="in_proj_qkv")

        cw, cb = conv_w[i], conv_b[i]
        ssd_consts = (
            _pad_groups(cw[:, :SSD_WIDTH]), _pad_groups(cb[None, :SSD_WIDTH]),
            cw[:, SSD_WIDTH:], cb[None, SSD_WIDTH:],
            _pad_last(dt_bias[i][None, :], LANES),
            _pad_last(a_log[i][None, :], LANES),
            _pad_groups(jnp.repeat(d_skip[i], SSD_HEAD_DIM)[None, :]),
            _pad_groups(ssd_norm_w[i][None, :]),
            head_to_feat, head_to_tile)
        mixed = _ssd(zx, bc, dt, ssd_consts, batch=batch, seq=seq, out_width=SSD_WIDTH + ATT_WIDTH)
        mixed = _dilated_attention(qkv, mixed, batch=batch, seq=seq)
        mix = _matmul(mixed, w_out[i].astype(BF16), F32, name="out_proj")
        h, xn = _postnorm(mix, h, mix_post_w[i], 1.0, ffn2_pre_w[i])

        h, xn = _ffn(h, xn, ffn2_post_w[i], ffn2_w_gate[i], ffn2_w_up[i], ffn2_w_down[i], ple_pre_w[i])

        gate = _matmul(xn, w_ple_gate[i].astype(BF16), F32, name="ple_gate")
        h = _ple(gate, p[i].reshape(m, -1), w_ple_proj[i].astype(BF16), h, ple_post_w[i])

    return h.reshape(batch, seq, d_model)
```

```python
import functools
import math

import numpy as np
import jax
import jax.numpy as jnp
from jax import lax
from jax.experimental import pallas as pl
from jax.experimental.pallas import tpu as pltpu

F32 = jnp.float32
BF16 = jnp.bfloat16

RMS_EPS = 1e-6

LANES = 128
SUBLANES = 8
VMEM_LIMIT_BYTES = 58 * 1024 * 1024

SSD_HEAD_DIM = 64
SSD_HEADS = 40
SSD_GROUPS = 8
SSD_HEADS_PER_GROUP = SSD_HEADS // SSD_GROUPS
SSD_GROUP_WIDTH = SSD_HEADS_PER_GROUP * SSD_HEAD_DIM
SSD_GROUP_PAD = 384
SSD_WIDTH = SSD_HEADS * SSD_HEAD_DIM
SSD_WIDTH_PAD = SSD_GROUPS * SSD_GROUP_PAD
SSD_STATE = 128
SSD_CONV = 4
SSD_CHUNK = 128
SSD_BC_WIDTH = 2 * SSD_GROUPS * SSD_STATE
CONV_HALO_ROWS = SUBLANES

ATT_HEAD_DIM = 128
ATT_HEADS = 12
ATT_WIDTH = ATT_HEADS * ATT_HEAD_DIM
ATT_BRANCHES = ((128, 1), (512, 4), (2048, 16))
ATT_BLOCK = 128
ATT_QUERY_BLOCK = 512
ATT_MERGE_ROWS = 256


def _cparams(semantics):
    return pltpu.CompilerParams(dimension_semantics=semantics, vmem_limit_bytes=VMEM_LIMIT_BYTES)


def _dot(a, b):
    return jnp.dot(a, b, preferred_element_type=F32)


def _dot_nt(a, b):
    return lax.dot_general(a, b, (((1,), (1,)), ((), ())), preferred_element_type=F32)


def _dot_tn(a, b):
    return lax.dot_general(a, b, (((0,), (0,)), ((), ())), preferred_element_type=F32)


def _split_bf16(x):
    hi = x.astype(BF16)
    lo = (x - hi.astype(F32)).astype(BF16)
    return hi, lo


def _dot_split_lhs(x, sel):
    hi, lo = _split_bf16(x)
    return _dot(hi, sel) + _dot(lo, sel)


def _silu(x):
    hx = 0.5 * x
    return hx + hx * jnp.tanh(hx)


def _rms_scale(v):
    return lax.rsqrt(jnp.mean(v * v, axis=-1, keepdims=True) + RMS_EPS)


def _row_slabs(n_rows, rows_per_step, body, unroll=1):
    def step(r, carry):
        body(pl.ds(pl.multiple_of(r * rows_per_step, rows_per_step), rows_per_step))
        return carry

    lax.fori_loop(0, n_rows // rows_per_step, step, 0, unroll=unroll)


def _rmsnorm_kernel(x_ref, w_ref, o_ref):
    def body(rows):
        x = x_ref[rows, :]
        o_ref[rows, :] = (x * _rms_scale(x) * w_ref[...]).astype(o_ref.dtype)

    _row_slabs(x_ref.shape[0], 16, body, unroll=4)


def _rmsnorm(x, w, *, tm=256):
    m, d = x.shape
    return pl.pallas_call(
        _rmsnorm_kernel,
        out_shape=jax.ShapeDtypeStruct((m, d), BF16),
        grid=(m // tm,),
        in_specs=[pl.BlockSpec((tm, d), lambda i: (i, 0)),
                  pl.BlockSpec((1, d), lambda i: (0, 0))],
        out_specs=pl.BlockSpec((tm, d), lambda i: (i, 0)),
        compiler_params=_cparams(("parallel",)),
        name="rmsnorm",
    )(x, w.reshape(1, d))


def _matmul_kernel(a_ref, w_ref, o_ref, *, k_steps):
    part = _dot(a_ref[...], w_ref[...].astype(BF16))
    if k_steps == 1:
        o_ref[...] = part.astype(o_ref.dtype)
    else:
        k = pl.program_id(2)

        @pl.when(k == 0)
        def _():
            o_ref[...] = part

        @pl.when(k > 0)
        def _():
            o_ref[...] += part


def _matmul(a, w, out_dtype, *, tm=1024, tn=512, k_steps=1, w_cols=None, name="matmul"):
    m, k = a.shape
    start, n = (0, w.shape[1]) if w_cols is None else w_cols
    tn = min(tn, n)
    tk = k // k_steps
    assert k_steps == 1 or out_dtype == F32
    assert start % tn == 0 and n % tn == 0
    first = start // tn
    return pl.pallas_call(
        functools.partial(_matmul_kernel, k_steps=k_steps),
        out_shape=jax.ShapeDtypeStruct((m, n), out_dtype),
        grid=(m // tm, n // tn, k_steps),
        in_specs=[pl.BlockSpec((tm, tk), lambda i, j, kk: (i, kk)),
                  pl.BlockSpec((tk, tn), lambda i, j, kk: (kk, first + j))],
        out_specs=pl.BlockSpec((tm, tn), lambda i, j, kk: (i, j)),
        compiler_params=_cparams(("parallel", "arbitrary", "arbitrary")),
        name=name,
    )(a, w)


def _matmul_pad_groups_kernel(a_ref, w_ref, o_ref, *, groups_per_tile):
    res = _dot(a_ref[...], w_ref[...].astype(BF16))
    pad = jnp.zeros((res.shape[0], SSD_GROUP_PAD - SSD_GROUP_WIDTH), F32)
    parts = []
    for g in range(groups_per_tile):
        parts += [res[:, g * SSD_GROUP_WIDTH:(g + 1) * SSD_GROUP_WIDTH], pad]
    o_ref[...] = jnp.concatenate(parts, axis=1).astype(o_ref.dtype)


def _matmul_pad_groups(a, w, *, tm=1024, groups_per_tile=4, name="matmul_pad_groups"):
    m, k = a.shape
    n = w.shape[1]
    tn_in = groups_per_tile * SSD_GROUP_WIDTH
    tn_out = groups_per_tile * SSD_GROUP_PAD
    assert n % tn_in == 0 and tn_in % LANES == 0
    return pl.pallas_call(
        functools.partial(_matmul_pad_groups_kernel, groups_per_tile=groups_per_tile),
        out_shape=jax.ShapeDtypeStruct((m, n // tn_in * tn_out), BF16),
        grid=(m // tm, n // tn_in),
        in_specs=[pl.BlockSpec((tm, k), lambda i, j: (i, 0)),
                  pl.BlockSpec((k, tn_in), lambda i, j: (0, j))],
        out_specs=pl.BlockSpec((tm, tn_out), lambda i, j: (i, j)),
        compiler_params=_cparams(("parallel", "arbitrary")),
        name=name,
    )(a, w)


def _split_w_in_kernel(w_ref, zx_ref, qkv_ref, *, zx_cols, q_start, q_scale):
    zx_ref[...] = w_ref[:, :zx_cols].astype(zx_ref.dtype)
    q_end = q_start + ATT_WIDTH
    qkv_ref[:, :ATT_WIDTH] = (w_ref[:, q_start:q_end] * q_scale).astype(qkv_ref.dtype)
    qkv_ref[:, ATT_WIDTH:] = w_ref[:, q_end:].astype(qkv_ref.dtype)


def _split_w_in(w, zx_cols, q_start, q_scale, *, tr=256):
    k, n = w.shape
    qkv_cols = n - q_start
    return pl.pallas_call(
        functools.partial(_split_w_in_kernel, zx_cols=zx_cols, q_start=q_start, q_scale=q_scale),
        out_shape=(jax.ShapeDtypeStruct((k, zx_cols), BF16), jax.ShapeDtypeStruct((k, qkv_cols), BF16)),
        grid=(k // tr,),
        in_specs=[pl.BlockSpec((tr, n), lambda r: (r, 0))],
        out_specs=(pl.BlockSpec((tr, zx_cols), lambda r: (r, 0)),
                   pl.BlockSpec((tr, qkv_cols), lambda r: (r, 0))),
        compiler_params=_cparams(("parallel",)),
        name="split_w_in",
    )(w)


def _gateup_kernel(a_ref, wg_ref, wu_ref, o_ref):
    a = a_ref[...]
    g = _dot(a, wg_ref[...].astype(BF16))
    u = _dot(a, wu_ref[...].astype(BF16))
    o_ref[...] = (_silu(g) * u).astype(o_ref.dtype)


def _gateup(a, wg, wu, *, tm=2048, tn=256):
    m, k = a.shape
    n = wg.shape[1]
    return pl.pallas_call(
        _gateup_kernel,
        out_shape=jax.ShapeDtypeStruct((m, n), BF16),
        grid=(m // tm, n // tn),
        in_specs=[pl.BlockSpec((tm, k), lambda i, j: (i, 0), pipeline_mode=pl.Buffered(1)),
                  pl.BlockSpec((k, tn), lambda i, j: (0, j)),
                  pl.BlockSpec((k, tn), lambda i, j: (0, j))],
        out_specs=pl.BlockSpec((tm, tn), lambda i, j: (i, j)),
        compiler_params=_cparams(("parallel", "arbitrary")),
        name="swiglu_gateup",
    )(a, wg, wu)


def _postnorm_kernel(f_ref, h_ref, nw_ref, next_w_ref, o_ref, xn_ref, *, scale):
    def body(rows):
        f = f_ref[rows, :]
        hn = h_ref[rows, :] + scale * (f * _rms_scale(f) * nw_ref[...])
        o_ref[rows, :] = hn
        xn_ref[rows, :] = (hn * _rms_scale(hn) * next_w_ref[...]).astype(xn_ref.dtype)

    _row_slabs(f_ref.shape[0], 16, body, unroll=4)


def _postnorm(f, h, nw, scale, next_w, *, tm=256):
    m, n = f.shape
    row_spec = pl.BlockSpec((tm, n), lambda i: (i, 0))
    vec_spec = pl.BlockSpec((1, n), lambda i: (0, 0))
    return pl.pallas_call(
        functools.partial(_postnorm_kernel, scale=scale),
        out_shape=(jax.ShapeDtypeStruct((m, n), F32), jax.ShapeDtypeStruct((m, n), BF16)),
        grid=(m // tm,),
        in_specs=[row_spec, row_spec, vec_spec, vec_spec],
        out_specs=(row_spec, row_spec),
        compiler_params=_cparams(("parallel",)),
        name="postnorm_residual",
    )(f, h, nw.reshape(1, n), next_w.reshape(1, n))


def _ple_kernel(g_ref, p_ref, wp_ref, h_ref, nw_ref, o_ref):
    def body(rows):
        proj = _dot(p_ref[rows, :].astype(BF16), wp_ref[...])
        f = jax.nn.sigmoid(g_ref[rows, :]) * proj
        o_ref[rows, :] = h_ref[rows, :] + f * _rms_scale(f) * nw_ref[...]

    _row_slabs(g_ref.shape[0], 64, body)


def _ple(g, p, wp, h, nw, *, tm=256):
    m, n = g.shape
    pdim = p.shape[1]
    row_spec = pl.BlockSpec((tm, n), lambda i: (i, 0))
    return pl.pallas_call(
        _ple_kernel,
        out_shape=jax.ShapeDtypeStruct((m, n), F32),
        grid=(m // tm,),
        in_specs=[row_spec,
                  pl.BlockSpec((tm, pdim), lambda i: (i, 0)),
                  pl.BlockSpec((pdim, n), lambda i: (0, 0)),
                  row_spec,
                  pl.BlockSpec((1, n), lambda i: (0, 0))],
        out_specs=row_spec,
        compiler_params=_cparams(("parallel",)),
        name="ple_gate_postnorm",
    )(g, p, wp, h, nw.reshape(1, n))


def _ssd_kernel(z_ref, x_ref, bc_ref, dt_ref, cwx_ref, cbx_ref, cwbc_ref, cbbc_ref, dtb_ref, alog_ref,
                dskip_ref, nw_ref, head_to_feat_ref, head_to_tile_ref, o_ref, xbuf, bcbuf, state):
    c = pl.program_id(1)
    q = SSD_CHUNK
    halo = CONV_HALO_ROWS

    @pl.when(c == 0)
    def _():
        xbuf[0:halo, :] = jnp.zeros((halo, xbuf.shape[1]), F32)
        bcbuf[0:halo, :] = jnp.zeros((halo, bcbuf.shape[1]), F32)
        state[...] = jnp.zeros(state.shape, F32)

    xbuf[halo:halo + q, :] = x_ref[...].astype(F32)
    bcbuf[halo:halo + q, :] = bc_ref[...].astype(F32)

    v = dt_ref[...] + dtb_ref[...]
    dt = jnp.maximum(v, 0.0) + jnp.log(1.0 + jnp.exp(-jnp.abs(v)))
    head_lane = lax.broadcasted_iota(jnp.int32, dt.shape, 1)
    dt = jnp.where(head_lane < SSD_HEADS, dt, 0.0)
    da = dt * -jnp.exp(alog_ref[...])
    row = lax.broadcasted_iota(jnp.int32, (q, q), 0)
    col = lax.broadcasted_iota(jnp.int32, (q, q), 1)
    causal = row >= col
    tril = jnp.where(causal, 1.0, 0.0).astype(BF16)
    da_hi, da_lo = _split_bf16(da)
    a_cs = _dot(tril, da_hi) + _dot(tril, da_lo)
    a_cs_t = a_cs.T
    head_to_feat = head_to_feat_ref[...]
    a_feat = _dot_split_lhs(a_cs, head_to_feat)
    dt_feat = _dot_split_lhs(dt, head_to_feat)
    a_tile = _dot_split_lhs(a_cs, head_to_tile_ref[...])

    lane = lax.broadcasted_iota(jnp.int32, (q, LANES), 1)
    heads_per_tile = LANES // SSD_HEAD_DIM

    for g in range(SSD_GROUPS):
        lo, hi = g * SSD_GROUP_PAD, (g + 1) * SSD_GROUP_PAD
        blo, clo = g * SSD_STATE, (SSD_GROUPS + g) * SSD_STATE

        def conv(buf, w_ref, b_ref, c0, c1):
            acc = b_ref[:, c0:c1] + w_ref[0:1, c0:c1] * buf[pl.ds(halo - 3, q), c0:c1]
            for t in range(1, SSD_CONV):
                acc = acc + w_ref[t:t + 1, c0:c1] * buf[pl.ds(halo - 3 + t, q), c0:c1]
            return _silu(acc)

        xg = conv(xbuf, cwx_ref, cbx_ref, lo, hi)
        bg = conv(bcbuf, cwbc_ref, cbbc_ref, blo, blo + SSD_STATE)
        cg = conv(bcbuf, cwbc_ref, cbbc_ref, clo, clo + SSD_STATE)
        bg16 = bg.astype(BF16)
        cg16 = cg.astype(BF16)

        a_g = a_feat[:, lo:hi]
        a_last = a_g[q - 1:q, :]
        xdt = xg * dt_feat[:, lo:hi]

        cb = _dot_nt(cg16, bg16)
        m_parts = []
        for jj in range(SSD_HEADS_PER_GROUP):
            j = g * SSD_HEADS_PER_GROUP + jj
            seg = a_tile[:, j * q:(j + 1) * q] - a_cs_t[j:j + 1, :]
            decay = jnp.exp(jnp.where(causal, seg, -jnp.inf))
            m_parts.append((cb * decay).astype(BF16))
        y_tiles = []
        for t in range(SSD_GROUP_PAD // LANES):
            xt = xdt[:, t * LANES:(t + 1) * LANES]
            heads = [jj for jj in range(t * heads_per_tile, (t + 1) * heads_per_tile) if jj < SSD_HEADS_PER_GROUP]
            if len(heads) == 1:
                y_tiles.append(_dot(m_parts[heads[0]], xt.astype(BF16)))
            else:
                xs = [jnp.where(lane // SSD_HEAD_DIM == jj % heads_per_tile, xt, 0.0).astype(BF16) for jj in heads]
                y_tiles.append(_dot(jnp.concatenate([m_parts[jj] for jj in heads], axis=1),
                                    jnp.concatenate(xs, axis=0)))
        y = jnp.concatenate(y_tiles, axis=1)

        prev = state[:, lo:hi]
        y = y + _dot(cg16, prev.astype(BF16)) * jnp.exp(a_g)
        xdd = (xdt * jnp.exp(a_last - a_g)).astype(BF16)
        state[:, lo:hi] = prev * jnp.exp(a_last) + _dot_tn(bg16, xdd)

        y = y + xg * dskip_ref[:, lo:hi]
        y = y * _silu(z_ref[:, lo:hi].astype(F32))
        ms = jnp.sum(y * y, axis=-1, keepdims=True) * (1.0 / SSD_GROUP_WIDTH)
        y = y * lax.rsqrt(ms + RMS_EPS) * nw_ref[:, lo:hi]
        o_ref[:, g * SSD_GROUP_WIDTH:(g + 1) * SSD_GROUP_WIDTH] = y[:, :SSD_GROUP_WIDTH].astype(o_ref.dtype)

    xbuf[0:halo, :] = xbuf[q:q + halo, :]
    bcbuf[0:halo, :] = bcbuf[q:q + halo, :]


def _ssd(zx, bc, dt, consts, *, batch, seq, out_width):
    (cwx, cbx, cwbc, cbbc, dtb, alog, dskip, nw, head_to_feat, head_to_tile) = consts
    q = SSD_CHUNK
    nc = seq // q
    m = batch * seq
    wp = SSD_WIDTH_PAD

    def tok(b, c):
        return b * nc + c

    def const_spec(arr):
        return pl.BlockSpec(arr.shape, lambda b, c: (0, 0))

    return pl.pallas_call(
        _ssd_kernel,
        out_shape=jax.ShapeDtypeStruct((m, out_width), BF16),
        grid=(batch, nc),
        in_specs=[pl.BlockSpec((q, wp), lambda b, c: (tok(b, c), 0)),
                  pl.BlockSpec((q, wp), lambda b, c: (tok(b, c), 1)),
                  pl.BlockSpec((q, SSD_BC_WIDTH), lambda b, c: (tok(b, c), 0)),
                  pl.BlockSpec((q, LANES), lambda b, c: (tok(b, c), 0)),
                  const_spec(cwx), const_spec(cbx), const_spec(cwbc), const_spec(cbbc),
                  const_spec(dtb), const_spec(alog), const_spec(dskip), const_spec(nw),
                  const_spec(head_to_feat), const_spec(head_to_tile)],
        out_specs=pl.BlockSpec((q, SSD_WIDTH), lambda b, c: (tok(b, c), 0)),
        scratch_shapes=[pltpu.VMEM((q + CONV_HALO_ROWS, wp), F32),
                        pltpu.VMEM((q + CONV_HALO_ROWS, SSD_BC_WIDTH), F32),
                        pltpu.VMEM((SSD_STATE, wp), F32)],
        compiler_params=_cparams(("parallel", "arbitrary")),
        name="ssd_chunk_scan",
    )(zx, zx, bc, dt, cwx, cbx, cwbc, cbbc, dtb, alog, dskip, nw, head_to_feat, head_to_tile)


def _alibi_slopes(n):
    def pow2(m):
        start = 2.0 ** (-8.0 / m)
        return [start ** (i + 1) for i in range(m)]
    if math.log2(n).is_integer():
        s = pow2(n)
    else:
        c = 2 ** int(math.floor(math.log2(n)))
        s = pow2(c) + pow2(2 * c)[0::2][: n - c]
    return [float(np.float32(v)) for v in s]


def _alibi_base(n_query, n_key, key_offset):
    qi = lax.broadcasted_iota(jnp.int32, (n_query, n_key), 0)
    kj = lax.broadcasted_iota(jnp.int32, (n_query, n_key), 1)
    dist = qi - kj + key_offset
    return jnp.where((dist >= 0) & (dist <= ATT_BLOCK), -dist.astype(F32), -jnp.inf)


def _attn_kernel(q_ref, k_ref, v_ref, slope_ref, mixed_ref, o_ref, o_acc, lse_acc):
    seq = q_ref.shape[0]
    c = ATT_BLOCK

    def attend(g, d, start, qb, bases):
        stride = None if d == 1 else d
        rows = pl.ds(start, qb, stride=stride)
        q = q_ref[rows, :].astype(BF16)
        tok_slope = slope_ref[0][:, :qb] * float(d)
        s_cur = _dot_nt(q, k_ref[rows, :].astype(BF16)) + tok_slope * bases[0]
        mx = jnp.max(s_cur, axis=-1, keepdims=True)
        if bases[1] is not None:
            prows = pl.ds(start - c * d, c, stride=stride)
            s_prev = _dot_nt(q, k_ref[prows, :].astype(BF16)) + tok_slope[:, :c] * bases[1]
            mx = jnp.maximum(mx, jnp.max(s_prev, axis=-1, keepdims=True))
        p_cur = jnp.exp(s_cur - mx)
        den = jnp.sum(p_cur, axis=-1, keepdims=True)
        o = _dot(p_cur.astype(BF16), v_ref[rows, :].astype(BF16))
        if bases[1] is not None:
            p_prev = jnp.exp(s_prev - mx)
            den = den + jnp.sum(p_prev, axis=-1, keepdims=True)
            o = o + _dot(p_prev.astype(BF16), v_ref[prows, :].astype(BF16))
        o_acc[g, rows, :] = o / den
        lse_acc[g, rows, :] = jnp.broadcast_to(mx + jnp.log(den), (qb, LANES))

    for g, (_, d) in enumerate(ATT_BRANCHES):
        sub_len = seq // d
        qb = min(ATT_QUERY_BLOCK, sub_len)
        n_blocks = sub_len // qb
        base_cur = _alibi_base(qb, qb, 0)
        base_prev = _alibi_base(qb, c, c) if n_blocks > 1 else None

        def residue(r, carry, g=g, d=d, qb=qb, n_blocks=n_blocks, base_cur=base_cur, base_prev=base_prev):
            attend(g, d, r, qb, (base_cur, None))

            def later(nb, inner):
                offset = nb * (qb * d)
                attend(g, d, r + (pl.multiple_of(offset, qb) if d == 1 else offset), qb, (base_cur, base_prev))
                return inner

            if n_blocks > 1:
                lax.fori_loop(1, n_blocks, later, 0)
            return carry

        if d == 1:
            residue(0, 0)
        else:
            lax.fori_loop(0, d, residue, 0, unroll=2 if n_blocks == 1 else 1)

    def merge(i, carry):
        rows = pl.ds(pl.multiple_of(i * ATT_MERGE_ROWS, ATT_MERGE_ROWS), ATT_MERGE_ROWS)
        lses = [lse_acc[g, rows, :] for g in range(len(ATT_BRANCHES))]
        mx = functools.reduce(jnp.maximum, lses)
        es = [jnp.exp(l - mx) for l in lses]
        num = sum(e * o_acc[g, rows, :] for g, e in enumerate(es))
        o_ref[rows, :] = (num / sum(es)).astype(o_ref.dtype)
        return carry

    lax.fori_loop(0, seq // ATT_MERGE_ROWS, merge, 0)


def _dilated_attention(qkv, mixed, *, batch, seq):
    e = ATT_HEAD_DIM
    n_branch = len(ATT_BRANCHES)
    assert all(seq % (d * ATT_BLOCK) == 0 for _, d in ATT_BRANCHES)
    first_col_block = (mixed.shape[1] - ATT_WIDTH) // e
    assert first_col_block * e + ATT_WIDTH == mixed.shape[1]
    slopes = np.repeat(np.asarray(_alibi_slopes(ATT_HEADS), np.float32)[:, None, None], ATT_QUERY_BLOCK, axis=2)

    def head_spec(part):
        return pl.BlockSpec((seq, e), lambda b, h: (b, part * ATT_HEADS + h))

    return pl.pallas_call(
        _attn_kernel,
        out_shape=jax.ShapeDtypeStruct(mixed.shape, mixed.dtype),
        grid=(batch, ATT_HEADS),
        in_specs=[head_spec(0), head_spec(1), head_spec(2),
                  pl.BlockSpec((1, 1, ATT_QUERY_BLOCK), lambda b, h: (h, 0, 0)),
                  pl.BlockSpec(memory_space=pl.ANY)],
        out_specs=pl.BlockSpec((seq, e), lambda b, h: (b, first_col_block + h)),
        input_output_aliases={4: 0},
        scratch_shapes=[pltpu.VMEM((n_branch, seq, e), F32),
                        pltpu.VMEM((n_branch, seq, LANES), F32)],
        compiler_params=_cparams(("parallel", "parallel")),
        name="dilated_attention",
    )(qkv, qkv, qkv, jnp.asarray(slopes), mixed)


def _pad_groups(t):
    lead = t.shape[:-1]
    t = t.reshape(*lead, SSD_GROUPS, SSD_GROUP_WIDTH)
    t = jnp.pad(t, [(0, 0)] * len(lead) + [(0, 0), (0, SSD_GROUP_PAD - SSD_GROUP_WIDTH)])
    return t.reshape(*lead, SSD_WIDTH_PAD)


def _pad_last(t, n):
    return jnp.pad(t, [(0, 0)] * (t.ndim - 1) + [(0, n - t.shape[-1])])


def _head_selectors():
    head_to_feat = np.zeros((LANES, SSD_WIDTH_PAD), np.float32)
    head_to_tile = np.zeros((LANES, SSD_HEADS * SSD_CHUNK), np.float32)
    for j in range(SSD_HEADS):
        g, jj = divmod(j, SSD_HEADS_PER_GROUP)
        f0 = g * SSD_GROUP_PAD + jj * SSD_HEAD_DIM
        head_to_feat[j, f0:f0 + SSD_HEAD_DIM] = 1.0
        head_to_tile[j, j * SSD_CHUNK:(j + 1) * SSD_CHUNK] = 1.0
    return jnp.asarray(head_to_feat, BF16), jnp.asarray(head_to_tile, BF16)


def _ffn(h, xn, post_w, w_gate, w_up, w_down, next_pre_w):
    act = _gateup(xn, w_gate, w_up)
    f = _matmul(act, w_down.astype(BF16), F32, tm=512, name="ffn_down")
    return _postnorm(f, h, post_w, 0.5, next_pre_w)


def kernel(x, p, ffn1_pre_w, ffn1_post_w, ffn1_w_gate, ffn1_w_up, ffn1_w_down, mix_pre_w, mix_post_w, w_in, conv_w, conv_b, dt_bias, a_log, d_skip, ssd_norm_w, w_out, ffn2_pre_w, ffn2_post_w, ffn2_w_gate, ffn2_w_up, ffn2_w_down, ple_pre_w, ple_post_w, w_ple_gate, w_ple_proj):
    batch, seq, d_model = x.shape
    depth = w_in.shape[0]
    m = batch * seq
    h = x.reshape(m, d_model)
    head_to_feat, head_to_tile = _head_selectors()
    i1 = SSD_WIDTH
    i2 = i1 + SSD_WIDTH
    i3 = i2 + SSD_BC_WIDTH
    i4 = i3 + SSD_HEADS

    for i in range(depth):
        xn = _rmsnorm(h, ffn1_pre_w[i])
        h, u = _ffn(h, xn, ffn1_post_w[i], ffn1_w_gate[i], ffn1_w_up[i], ffn1_w_down[i], mix_pre_w[i])

        wi = w_in[i]
        w_zx, w_qkv = _split_w_in(wi, i2, i4, ATT_HEAD_DIM ** -0.5)
        zx = _matmul_pad_groups(u, w_zx, name="in_proj_zx")
        bc = _matmul(u, wi, BF16, w_cols=(i2, SSD_BC_WIDTH), name="in_proj_bc")
        dt = _matmul(u, wi, F32, w_cols=(i3, LANES), name="in_proj_dt")
        qkv = _matmul(u, w_qkv, F32, name="in_proj_qkv")

        cw, cb = conv_w[i], conv_b[i]
        ssd_consts = (
            _pad_groups(cw[:, :SSD_WIDTH]), _pad_groups(cb[None, :SSD_WIDTH]),
            cw[:, SSD_WIDTH:], cb[None, SSD_WIDTH:],
            _pad_last(dt_bias[i][None, :], LANES),
            _pad_last(a_log[i][None, :], LANES),
            _pad_groups(jnp.repeat(d_skip[i], SSD_HEAD_DIM)[None, :]),
            _pad_groups(ssd_norm_w[i][None, :]),
            head_to_feat, head_to_tile)
        mixed = _ssd(zx, bc, dt, ssd_consts, batch=batch, seq=seq, out_width=SSD_WIDTH + ATT_WIDTH)
        mixed = _dilated_attention(qkv, mixed, batch=batch, seq=seq)
        mix = _matmul(mixed, w_out[i].astype(BF16), F32, name="out_proj")
        h, xn = _postnorm(mix, h, mix_post_w[i], 1.0, ffn2_pre_w[i])

        h, xn = _ffn(h, xn, ffn2_post_w[i], ffn2_w_gate[i], ffn2_w_up[i], ffn2_w_down[i], ple_pre_w[i])

        gate = _matmul(xn, w_ple_gate[i].astype(BF16), F32, name="ple_gate")
        h = _ple(gate, p[i].reshape(m, -1), w_ple_proj[i].astype(BF16), h, ple_post_w[i])

    return h.reshape(batch, seq, d_model)
```

```python
import functools
import math

import numpy as np
import jax
import jax.numpy as jnp
from jax import lax
from jax.experimental import pallas as pl
from jax.experimental.pallas import tpu as pltpu

F32 = jnp.float32
BF16 = jnp.bfloat16

RMS_EPS = 1e-6

LANES = 128
SUBLANES = 8
VMEM_LIMIT_BYTES = 58 * 1024 * 1024

SSD_HEAD_DIM = 64
SSD_HEADS = 40
SSD_GROUPS = 8
SSD_HEADS_PER_GROUP = SSD_HEADS // SSD_GROUPS
SSD_GROUP_WIDTH = SSD_HEADS_PER_GROUP * SSD_HEAD_DIM
SSD_GROUP_PAD = 384
SSD_WIDTH = SSD_HEADS * SSD_HEAD_DIM
SSD_WIDTH_PAD = SSD_GROUPS * SSD_GROUP_PAD
SSD_STATE = 128
SSD_CONV = 4
SSD_CHUNK = 128
SSD_BC_WIDTH = 2 * SSD_GROUPS * SSD_STATE
CONV_HALO_ROWS = SUBLANES

ATT_HEAD_DIM = 128
ATT_HEADS = 12
ATT_WIDTH = ATT_HEADS * ATT_HEAD_DIM
ATT_BRANCHES = ((128, 1), (512, 4), (2048, 16))
ATT_BLOCK = 128
ATT_QUERY_BLOCK = 512
ATT_MERGE_ROWS = 256


def _cparams(semantics):
    return pltpu.CompilerParams(dimension_semantics=semantics, vmem_limit_bytes=VMEM_LIMIT_BYTES)


def _dot(a, b):
    return jnp.dot(a, b, preferred_element_type=F32)


def _dot_nt(a, b):
    return lax.dot_general(a, b, (((1,), (1,)), ((), ())), preferred_element_type=F32)


def _dot_tn(a, b):
    return lax.dot_general(a, b, (((0,), (0,)), ((), ())), preferred_element_type=F32)


def _split_bf16(x):
    hi = x.astype(BF16)
    lo = (x - hi.astype(F32)).astype(BF16)
    return hi, lo


def _dot_split_lhs(x, sel):
    hi, lo = _split_bf16(x)
    return _dot(hi, sel) + _dot(lo, sel)


def _silu(x):
    hx = 0.5 * x
    return hx + hx * jnp.tanh(hx)


def _rms_scale(v):
    return lax.rsqrt(jnp.mean(v * v, axis=-1, keepdims=True) + RMS_EPS)


def _row_slabs(n_rows, rows_per_step, body, unroll=1):
    def step(r, carry):
        body(pl.ds(pl.multiple_of(r * rows_per_step, rows_per_step), rows_per_step))
        return carry

    lax.fori_loop(0, n_rows // rows_per_step, step, 0, unroll=unroll)


def _rmsnorm_kernel(x_ref, w_ref, o_ref):
    def body(rows):
        x = x_ref[rows, :]
        o_ref[rows, :] = (x * _rms_scale(x) * w_ref[...]).astype(o_ref.dtype)

    _row_slabs(x_ref.shape[0], 16, body, unroll=4)


def _rmsnorm(x, w, *, tm=256):
    m, d = x.shape
    return pl.pallas_call(
        _rmsnorm_kernel,
        out_shape=jax.ShapeDtypeStruct((m, d), BF16),
        grid=(m // tm,),
        in_specs=[pl.BlockSpec((tm, d), lambda i: (i, 0)),
                  pl.BlockSpec((1, d), lambda i: (0, 0))],
        out_specs=pl.BlockSpec((tm, d), lambda i: (i, 0)),
        compiler_params=_cparams(("parallel",)),
        name="rmsnorm",
    )(x, w.reshape(1, d))


def _matmul_kernel(a_ref, w_ref, o_ref, *, k_steps):
    part = _dot(a_ref[...], w_ref[...].astype(BF16))
    if k_steps == 1:
        o_ref[...] = part.astype(o_ref.dtype)
    else:
        k = pl.program_id(2)

        @pl.when(k == 0)
        def _():
            o_ref[...] = part

        @pl.when(k > 0)
        def _():
            o_ref[...] += part


def _matmul(a, w, out_dtype, *, tm=1024, tn=512, k_steps=1, w_cols=None, name="matmul"):
    m, k = a.shape
    start, n = (0, w.shape[1]) if w_cols is None else w_cols
    tn = min(tn, n)
    tk = k // k_steps
    assert k_steps == 1 or out_dtype == F32
    assert start % tn == 0 and n % tn == 0
    first = start // tn
    return pl.pallas_call(
        functools.partial(_matmul_kernel, k_steps=k_steps),
        out_shape=jax.ShapeDtypeStruct((m, n), out_dtype),
        grid=(m // tm, n // tn, k_steps),
        in_specs=[pl.BlockSpec((tm, tk), lambda i, j, kk: (i, kk)),
                  pl.BlockSpec((tk, tn), lambda i, j, kk: (kk, first + j))],
        out_specs=pl.BlockSpec((tm, tn), lambda i, j, kk: (i, j)),
        compiler_params=_cparams(("parallel", "arbitrary", "arbitrary")),
        name=name,
    )(a, w)


def _matmul_pad_groups_kernel(a_ref, w_ref, o_ref, *, groups_per_tile):
    res = _dot(a_ref[...], w_ref[...].astype(BF16))
    pad = jnp.zeros((res.shape[0], SSD_GROUP_PAD - SSD_GROUP_WIDTH), F32)
    parts = []
    for g in range(groups_per_tile):
        parts += [res[:, g * SSD_GROUP_WIDTH:(g + 1) * SSD_GROUP_WIDTH], pad]
    o_ref[...] = jnp.concatenate(parts, axis=1).astype(o_ref.dtype)


def _matmul_pad_groups(a, w, *, tm=1024, groups_per_tile=4, name="matmul_pad_groups"):
    m, k = a.shape
    n = w.shape[1]
    tn_in = groups_per_tile * SSD_GROUP_WIDTH
    tn_out = groups_per_tile * SSD_GROUP_PAD
    assert n % tn_in == 0 and tn_in % LANES == 0
    return pl.pallas_call(
        functools.partial(_matmul_pad_groups_kernel, groups_per_tile=groups_per_tile),
        out_shape=jax.ShapeDtypeStruct((m, n // tn_in * tn_out), BF16),
        grid=(m // tm, n // tn_in),
        in_specs=[pl.BlockSpec((tm, k), lambda i, j: (i, 0)),
                  pl.BlockSpec((k, tn_in), lambda i, j: (0, j))],
        out_specs=pl.BlockSpec((tm, tn_out), lambda i, j: (i, j)),
        compiler_params=_cparams(("parallel", "arbitrary")),
        name=name,
    )(a, w)


def _split_w_in_kernel(w_ref, zx_ref, qkv_ref, *, zx_cols, q_start, q_scale):
    zx_ref[...] = w_ref[:, :zx_cols].astype(zx_ref.dtype)
    q_end = q_start + ATT_WIDTH
    qkv_ref[:, :ATT_WIDTH] = (w_ref[:, q_start:q_end] * q_scale).astype(qkv_ref.dtype)
    qkv_ref[:, ATT_WIDTH:] = w_ref[:, q_end:].astype(qkv_ref.dtype)


def _split_w_in(w, zx_cols, q_start, q_scale, *, tr=256):
    k, n = w.shape
    qkv_cols = n - q_start
    return pl.pallas_call(
        functools.partial(_split_w_in_kernel, zx_cols=zx_cols, q_start=q_start, q_scale=q_scale),
        out_shape=(jax.ShapeDtypeStruct((k, zx_cols), BF16), jax.ShapeDtypeStruct((k, qkv_cols), BF16)),
        grid=(k // tr,),
        in_specs=[pl.BlockSpec((tr, n), lambda r: (r, 0))],
        out_specs=(pl.BlockSpec((tr, zx_cols), lambda r: (r, 0)),
                   pl.BlockSpec((tr, qkv_cols), lambda r: (r, 0))),
        compiler_params=_cparams(("parallel",)),
        name="split_w_in",
    )(w)


def _gateup_kernel(a_ref, wg_ref, wu_ref, o_ref):
    a = a_ref[...]
    g = _dot(a, wg_ref[...].astype(BF16))
    u = _dot(a, wu_ref[...].astype(BF16))
    o_ref[...] = (_silu(g) * u).astype(o_ref.dtype)


def _gateup(a, wg, wu, *, tm=2048, tn=256):
    m, k = a.shape
    n = wg.shape[1]
    return pl.pallas_call(
        _gateup_kernel,
        out_shape=jax.ShapeDtypeStruct((m, n), BF16),
        grid=(m // tm, n // tn),
        in_specs=[pl.BlockSpec((tm, k), lambda i, j: (i, 0), pipeline_mode=pl.Buffered(1)),
                  pl.BlockSpec((k, tn), lambda i, j: (0, j)),
                  pl.BlockSpec((k, tn), lambda i, j: (0, j))],
        out_specs=pl.BlockSpec((tm, tn), lambda i, j: (i, j)),
        compiler_params=_cparams(("parallel", "arbitrary")),
        name="swiglu_gateup",
    )(a, wg, wu)


def _postnorm_kernel(f_ref, h_ref, nw_ref, next_w_ref, o_ref, xn_ref, *, scale):
    def body(rows):
        f = f_ref[rows, :]
        hn = h_ref[rows, :] + scale * (f * _rms_scale(f) * nw_ref[...])
        o_ref[rows, :] = hn
        xn_ref[rows, :] = (hn * _rms_scale(hn) * next_w_ref[...]).astype(xn_ref.dtype)

    _row_slabs(f_ref.shape[0], 16, body, unroll=4)


def _postnorm(f, h, nw, scale, next_w, *, tm=256):
    m, n = f.shape
    row_spec = pl.BlockSpec((tm, n), lambda i: (i, 0))
    vec_spec = pl.BlockSpec((1, n), lambda i: (0, 0))
    return pl.pallas_call(
        functools.partial(_postnorm_kernel, scale=scale),
        out_shape=(jax.ShapeDtypeStruct((m, n), F32), jax.ShapeDtypeStruct((m, n), BF16)),
        grid=(m // tm,),
        in_specs=[row_spec, row_spec, vec_spec, vec_spec],
        out_specs=(row_spec, row_spec),
        compiler_params=_cparams(("parallel",)),
        name="postnorm_residual",
    )(f, h, nw.reshape(1, n), next_w.reshape(1, n))


def _ple_kernel(g_ref, p_ref, wp_ref, h_ref, nw_ref, o_ref):
    def body(rows):
        proj = _dot(p_ref[rows, :].astype(BF16), wp_ref[...])
        f = jax.nn.sigmoid(g_ref[rows, :]) * proj
        o_ref[rows, :] = h_ref[rows, :] + f * _rms_scale(f) * nw_ref[...]

    _row_slabs(g_ref.shape[0], 64, body)


def _ple(g, p, wp, h, nw, *, tm=256):
    m, n = g.shape
    pdim = p.shape[1]
    row_spec = pl.BlockSpec((tm, n), lambda i: (i, 0))
    return pl.pallas_call(
        _ple_kernel,
        out_shape=jax.ShapeDtypeStruct((m, n), F32),
        grid=(m // tm,),
        in_specs=[row_spec,
                  pl.BlockSpec((tm, pdim), lambda i: (i, 0)),
                  pl.BlockSpec((pdim, n), lambda i: (0, 0)),
                  row_spec,
                  pl.BlockSpec((1, n), lambda i: (0, 0))],
        out_specs=row_spec,
        compiler_params=_cparams(("parallel",)),
        name="ple_gate_postnorm",
    )(g, p, wp, h, nw.reshape(1, n))


def _ssd_kernel(z_ref, x_ref, bc_ref, dt_ref, cwx_ref, cbx_ref, cwbc_ref, cbbc_ref, dtb_ref, alog_ref,
                dskip_ref, nw_ref, head_to_feat_ref, head_to_tile_ref, o_ref, xbuf, bcbuf, state):
    c = pl.program_id(1)
    q = SSD_CHUNK
    halo = CONV_HALO_ROWS

    @pl.when(c == 0)
    def _():
        xbuf[0:halo, :] = jnp.zeros((halo, xbuf.shape[1]), F32)
        bcbuf[0:halo, :] = jnp.zeros((halo, bcbuf.shape[1]), F32)
        state[...] = jnp.zeros(state.shape, F32)

    xbuf[halo:halo + q, :] = x_ref[...].astype(F32)
    bcbuf[halo:halo + q, :] = bc_ref[...].astype(F32)

    v = dt_ref[...] + dtb_ref[...]
    dt = jnp.maximum(v, 0.0) + jnp.log(1.0 + jnp.exp(-jnp.abs(v)))
    head_lane = lax.broadcasted_iota(jnp.int32, dt.shape, 1)
    dt = jnp.where(head_lane < SSD_HEADS, dt, 0.0)
    da = dt * -jnp.exp(alog_ref[...])
    row = lax.broadcasted_iota(jnp.int32, (q, q), 0)
    col = lax.broadcasted_iota(jnp.int32, (q, q), 1)
    causal = row >= col
    tril = jnp.where(causal, 1.0, 0.0).astype(BF16)
    da_hi, da_lo = _split_bf16(da)
    a_cs = _dot(tril, da_hi) + _dot(tril, da_lo)
    a_cs_t = a_cs.T
    head_to_feat = head_to_feat_ref[...]
    a_feat = _dot_split_lhs(a_cs, head_to_feat)
    dt_feat = _dot_split_lhs(dt, head_to_feat)
    a_tile = _dot_split_lhs(a_cs, head_to_tile_ref[...])

    lane = lax.broadcasted_iota(jnp.int32, (q, LANES), 1)
    heads_per_tile = LANES // SSD_HEAD_DIM

    for g in range(SSD_GROUPS):
        lo, hi = g * SSD_GROUP_PAD, (g + 1) * SSD_GROUP_PAD
        blo, clo = g * SSD_STATE, (SSD_GROUPS + g) * SSD_STATE

        def conv(buf, w_ref, b_ref, c0, c1):
            acc = b_ref[:, c0:c1] + w_ref[0:1, c0:c1] * buf[pl.ds(halo - 3, q), c0:c1]
            for t in range(1, SSD_CONV):
                acc = acc + w_ref[t:t + 1, c0:c1] * buf[pl.ds(halo - 3 + t, q), c0:c1]
            return _silu(acc)

        xg = conv(xbuf, cwx_ref, cbx_ref, lo, hi)
        bg = conv(bcbuf, cwbc_ref, cbbc_ref, blo, blo + SSD_STATE)
        cg = conv(bcbuf, cwbc_ref, cbbc_ref, clo, clo + SSD_STATE)
        bg16 = bg.astype(BF16)
        cg16 = cg.astype(BF16)

        a_g = a_feat[:, lo:hi]
        a_last = a_g[q - 1:q, :]
        xdt = xg * dt_feat[:, lo:hi]

        cb = _dot_nt(cg16, bg16)
        m_parts = []
        for jj in range(SSD_HEADS_PER_GROUP):
            j = g * SSD_HEADS_PER_GROUP + jj
            seg = a_tile[:, j * q:(j + 1) * q] - a_cs_t[j:j + 1, :]
            decay = jnp.exp(jnp.where(causal, seg, -jnp.inf))
            m_parts.append((cb * decay).astype(BF16))
        y_tiles = []
        for t in range(SSD_GROUP_PAD // LANES):
            xt = xdt[:, t * LANES:(t + 1) * LANES]
            heads = [jj for jj in range(t * heads_per_tile, (t + 1) * heads_per_tile) if jj < SSD_HEADS_PER_GROUP]
            if len(heads) == 1:
                y_tiles.append(_dot(m_parts[heads[0]], xt.astype(BF16)))
            else:
                xs = [jnp.where(lane // SSD_HEAD_DIM == jj % heads_per_tile, xt, 0.0).astype(BF16) for jj in heads]
                y_tiles.append(_dot(jnp.concatenate([m_parts[jj] for jj in heads], axis=1),
                                    jnp.concatenate(xs, axis=0)))
        y = jnp.concatenate(y_tiles, axis=1)

        prev = state[:, lo:hi]
        y = y + _dot(cg16, prev.astype(BF16)) * jnp.exp(a_g)
        xdd = (xdt * jnp.exp(a_last - a_g)).astype(BF16)
        state[:, lo:hi] = prev * jnp.exp(a_last) + _dot_tn(bg16, xdd)

        y = y + xg * dskip_ref[:, lo:hi]
        y = y * _silu(z_ref[:, lo:hi].astype(F32))
        ms = jnp.sum(y * y, axis=-1, keepdims=True) * (1.0 / SSD_GROUP_WIDTH)
        y = y * lax.rsqrt(ms + RMS_EPS) * nw_ref[:, lo:hi]
        o_ref[:, g * SSD_GROUP_WIDTH:(g + 1) * SSD_GROUP_WIDTH] = y[:, :SSD_GROUP_WIDTH].astype(o_ref.dtype)

    o_ref[:, SSD_WIDTH:] = jnp.zeros((q, o_ref.shape[1] - SSD_WIDTH), o_ref.dtype)

    xbuf[0:halo, :] = xbuf[q:q + halo, :]
    bcbuf[0:halo, :] = bcbuf[q:q + halo, :]


def _ssd(zx, bc, dt, consts, *, batch, seq, out_width):
    (cwx, cbx, cwbc, cbbc, dtb, alog, dskip, nw, head_to_feat, head_to_tile) = consts
    q = SSD_CHUNK
    nc = seq // q
    m = batch * seq
    wp = SSD_WIDTH_PAD

    def tok(b, c):
        return b * nc + c

    def const_spec(arr):
        return pl.BlockSpec(arr.shape, lambda b, c: (0, 0))

    return pl.pallas_call(
        _ssd_kernel,
        out_shape=jax.ShapeDtypeStruct((m, out_width), BF16),
        grid=(batch, nc),
        in_specs=[pl.BlockSpec((q, wp), lambda b, c: (tok(b, c), 0)),
                  pl.BlockSpec((q, wp), lambda b, c: (tok(b, c), 1)),
                  pl.BlockSpec((q, SSD_BC_WIDTH), lambda b, c: (tok(b, c), 0)),
                  pl.BlockSpec((q, LANES), lambda b, c: (tok(b, c), 0)),
                  const_spec(cwx), const_spec(cbx), const_spec(cwbc), const_spec(cbbc),
                  const_spec(dtb), const_spec(alog), const_spec(dskip), const_spec(nw),
                  const_spec(head_to_feat), const_spec(head_to_tile)],
        out_specs=pl.BlockSpec((q, out_width), lambda b, c: (tok(b, c), 0)),
        scratch_shapes=[pltpu.VMEM((q + CONV_HALO_ROWS, wp), F32),
                        pltpu.VMEM((q + CONV_HALO_ROWS, SSD_BC_WIDTH), F32),
                        pltpu.VMEM((SSD_STATE, wp), F32)],
        compiler_params=_cparams(("parallel", "arbitrary")),
        name="ssd_chunk_scan",
    )(zx, zx, bc, dt, cwx, cbx, cwbc, cbbc, dtb, alog, dskip, nw, head_to_feat, head_to_tile)


def _alibi_slopes(n):
    def pow2(m):
        start = 2.0 ** (-8.0 / m)
        return [start ** (i + 1) for i in range(m)]
    if math.log2(n).is_integer():
        s = pow2(n)
    else:
        c = 2 ** int(math.floor(math.log2(n)))
        s = pow2(c) + pow2(2 * c)[0::2][: n - c]
    return [float(np.float32(v)) for v in s]


def _alibi_base(n_query, n_key, key_offset):
    qi = lax.broadcasted_iota(jnp.int32, (n_query, n_key), 0)
    kj = lax.broadcasted_iota(jnp.int32, (n_query, n_key), 1)
    dist = qi - kj + key_offset
    return jnp.where((dist >= 0) & (dist <= ATT_BLOCK), -dist.astype(F32), -jnp.inf)


def _attn_kernel(q_ref, k_ref, v_ref, slope_ref, mixed_ref, o_ref, o_acc, lse_acc):
    seq = q_ref.shape[0]
    c = ATT_BLOCK

    def attend(g, d, start, qb, bases):
        stride = None if d == 1 else d
        rows = pl.ds(start, qb, stride=stride)
        q = q_ref[rows, :].astype(BF16)
        tok_slope = slope_ref[0][:, :qb] * float(d)
        s_cur = _dot_nt(q, k_ref[rows, :].astype(BF16)) + tok_slope * bases[0]
        mx = jnp.max(s_cur, axis=-1, keepdims=True)
        if bases[1] is not None:
            prows = pl.ds(start - c * d, c, stride=stride)
            s_prev = _dot_nt(q, k_ref[prows, :].astype(BF16)) + tok_slope[:, :c] * bases[1]
            mx = jnp.maximum(mx, jnp.max(s_prev, axis=-1, keepdims=True))
        p_cur = jnp.exp(s_cur - mx)
        den = jnp.sum(p_cur, axis=-1, keepdims=True)
        o = _dot(p_cur.astype(BF16), v_ref[rows, :].astype(BF16))
        if bases[1] is not None:
            p_prev = jnp.exp(s_prev - mx)
            den = den + jnp.sum(p_prev, axis=-1, keepdims=True)
            o = o + _dot(p_prev.astype(BF16), v_ref[prows, :].astype(BF16))
        o_acc[g, rows, :] = o / den
        lse_acc[g, rows, :] = jnp.broadcast_to(mx + jnp.log(den), (qb, LANES))

    for g, (_, d) in enumerate(ATT_BRANCHES):
        sub_len = seq // d
        qb = min(ATT_QUERY_BLOCK, sub_len)
        n_blocks = sub_len // qb
        base_cur = _alibi_base(qb, qb, 0)
        base_prev = _alibi_base(qb, c, c) if n_blocks > 1 else None

        def residue(r, carry, g=g, d=d, qb=qb, n_blocks=n_blocks, base_cur=base_cur, base_prev=base_prev):
            attend(g, d, r, qb, (base_cur, None))

            def later(nb, inner):
                offset = nb * (qb * d)
                attend(g, d, r + (pl.multiple_of(offset, qb) if d == 1 else offset), qb, (base_cur, base_prev))
                return inner

            if n_blocks > 1:
                lax.fori_loop(1, n_blocks, later, 0)
            return carry

        if d == 1:
            residue(0, 0)
        else:
            lax.fori_loop(0, d, residue, 0, unroll=2 if n_blocks == 1 else 1)

    def merge(i, carry):
        rows = pl.ds(pl.multiple_of(i * ATT_MERGE_ROWS, ATT_MERGE_ROWS), ATT_MERGE_ROWS)
        lses = [lse_acc[g, rows, :] for g in range(len(ATT_BRANCHES))]
        mx = functools.reduce(jnp.maximum, lses)
        es = [jnp.exp(l - mx) for l in lses]
        num = sum(e * o_acc[g, rows, :] for g, e in enumerate(es))
        o_ref[rows, :] = (num / sum(es)).astype(o_ref.dtype)
        return carry

    lax.fori_loop(0, seq // ATT_MERGE_ROWS, merge, 0)


def _dilated_attention(qkv, mixed, *, batch, seq):
    e = ATT_HEAD_DIM
    n_branch = len(ATT_BRANCHES)
    assert all(seq % (d * ATT_BLOCK) == 0 for _, d in ATT_BRANCHES)
    first_col_block = (mixed.shape[1] - ATT_WIDTH) // e
    assert first_col_block * e + ATT_WIDTH == mixed.shape[1]
    slopes = np.repeat(np.asarray(_alibi_slopes(ATT_HEADS), np.float32)[:, None, None], ATT_QUERY_BLOCK, axis=2)

    def head_spec(part):
        return pl.BlockSpec((seq, e), lambda b, h: (b, part * ATT_HEADS + h))

    return pl.pallas_call(
        _attn_kernel,
        out_shape=jax.ShapeDtypeStruct(mixed.shape, mixed.dtype),
        grid=(batch, ATT_HEADS),
        in_specs=[head_spec(0), head_spec(1), head_spec(2),
                  pl.BlockSpec((1, 1, ATT_QUERY_BLOCK), lambda b, h: (h, 0, 0)),
                  pl.BlockSpec(memory_space=pl.ANY)],
        out_specs=pl.BlockSpec((seq, e), lambda b, h: (b, first_col_block + h)),
        input_output_aliases={4: 0},
        scratch_shapes=[pltpu.VMEM((n_branch, seq, e), F32),
                        pltpu.VMEM((n_branch, seq, LANES), F32)],
        compiler_params=_cparams(("parallel", "parallel")),
        name="dilated_attention",
    )(qkv, qkv, qkv, jnp.asarray(slopes), mixed)


def _pad_groups(t):
    lead = t.shape[:-1]
    t = t.reshape(*lead, SSD_GROUPS, SSD_GROUP_WIDTH)
    t = jnp.pad(t, [(0, 0)] * len(lead) + [(0, 0), (0, SSD_GROUP_PAD - SSD_GROUP_WIDTH)])
    return t.reshape(*lead, SSD_WIDTH_PAD)


def _pad_last(t, n):
    return jnp.pad(t, [(0, 0)] * (t.ndim - 1) + [(0, n - t.shape[-1])])


def _head_selectors():
    head_to_feat = np.zeros((LANES, SSD_WIDTH_PAD), np.float32)
    head_to_tile = np.zeros((LANES, SSD_HEADS * SSD_CHUNK), np.float32)
    for j in range(SSD_HEADS):
        g, jj = divmod(j, SSD_HEADS_PER_GROUP)
        f0 = g * SSD_GROUP_PAD + jj * SSD_HEAD_DIM
        head_to_feat[j, f0:f0 + SSD_HEAD_DIM] = 1.0
        head_to_tile[j, j * SSD_CHUNK:(j + 1) * SSD_CHUNK] = 1.0
    return jnp.asarray(head_to_feat, BF16), jnp.asarray(head_to_tile, BF16)


def _ffn(h, xn, post_w, w_gate, w_up, w_down, next_pre_w):
    act = _gateup(xn, w_gate, w_up)
    f = _matmul(act, w_down.astype(BF16), F32, tm=512, name="ffn_down")
    return _postnorm(f, h, post_w, 0.5, next_pre_w)


def kernel(x, p, ffn1_pre_w, ffn1_post_w, ffn1_w_gate, ffn1_w_up, ffn1_w_down, mix_pre_w, mix_post_w, w_in, conv_w, conv_b, dt_bias, a_log, d_skip, ssd_norm_w, w_out, ffn2_pre_w, ffn2_post_w, ffn2_w_gate, ffn2_w_up, ffn2_w_down, ple_pre_w, ple_post_w, w_ple_gate, w_ple_proj):
    batch, seq, d_model = x.shape
    depth = w_in.shape[0]
    m = batch * seq
    h = x.reshape(m, d_model)
    head_to_feat, head_to_tile = _head_selectors()
    i1 = SSD_WIDTH
    i2 = i1 + SSD_WIDTH
    i3 = i2 + SSD_BC_WIDTH
    i4 = i3 + SSD_HEADS

    for i in range(depth):
        xn = _rmsnorm(h, ffn1_pre_w[i])
        h, u = _ffn(h, xn, ffn1_post_w[i], ffn1_w_gate[i], ffn1_w_up[i], ffn1_w_down[i], mix_pre_w[i])

        wi = w_in[i]
        w_zx, w_qkv = _split_w_in(wi, i2, i4, ATT_HEAD_DIM ** -0.5)
        zx = _matmul_pad_groups(u, w_zx, name="in_proj_zx")
        bc = _matmul(u, wi, BF16, w_cols=(i2, SSD_BC_WIDTH), name="in_proj_bc")
        dt = _matmul(u, wi, F32, w_cols=(i3, LANES), name="in_proj_dt")
        qkv = _matmul(u, w_qkv, F32, name="in_proj_qkv")

        cw, cb = conv_w[i], conv_b[i]
        ssd_consts = (
            _pad_groups(cw[:, :SSD_WIDTH]), _pad_groups(cb[None, :SSD_WIDTH]),
            cw[:, SSD_WIDTH:], cb[None, SSD_WIDTH:],
            _pad_last(dt_bias[i][None, :], LANES),
            _pad_last(a_log[i][None, :], LANES),
            _pad_groups(jnp.repeat(d_skip[i], SSD_HEAD_DIM)[None, :]),
            _pad_groups(ssd_norm_w[i][None, :]),
            head_to_feat, head_to_tile)
        mixed = _ssd(zx, bc, dt, ssd_consts, batch=batch, seq=seq, out_width=SSD_WIDTH + ATT_WIDTH)
        mixed = _dilated_attention(qkv, mixed, batch=batch, seq=seq)
        mix = _matmul(mixed, w_out[i].astype(BF16), F32, name="out_proj")
        h, xn = _postnorm(mix, h, mix_post_w[i], 1.0, ffn2_pre_w[i])

        h, xn = _ffn(h, xn, ffn2_post_w[i], ffn2_w_gate[i], ffn2_w_up[i], ffn2_w_down[i], ple_pre_w[i])

        gate = _matmul(xn, w_ple_gate[i].astype(BF16), F32, name="ple_gate")
        h = _ple(gate, p[i].reshape(m, -1), w_ple_proj[i].astype(BF16), h, ple_post_w[i])

    return h.reshape(batch, seq, d_model)
```
